```python
import math
import jax, jax.numpy as jnp
from jax import lax
import numpy as np

D_MODEL = 1024
BATCH = 8
SEQ = 2048
DEPTH = 1
DEC_BATCH = 128
DEC_SEQ = 8
PAST_LEN = 16384
PAGE_SIZE = 128

GMLP_W = D_MODEL
GMLP_GROUPS = 4
GMLP_GROUP_DIM = GMLP_W // GMLP_GROUPS
GMLP_CHUNK = 128
MLSTM_W = D_MODEL
MLSTM_HEADS = 4
MLSTM_HEAD_DIM = MLSTM_W // MLSTM_HEADS
MLSTM_CHUNK = 128
CONV_W = 4
D_FF = 2816
N_ADA = 9
EPS = 1e-6
IN_COLS = 2 * GMLP_W + 4 * MLSTM_W + 2 * MLSTM_HEADS + 2 * D_MODEL

kernel_name = 'gated_gmlp_mlstm_macaron_adaln_decoder_step'


def rmsnorm(x, g):
    xf = x.astype(jnp.float32)
    y = xf * lax.rsqrt(jnp.mean(xf * xf, axis=-1, keepdims=True) + EPS)
    return (y * g.astype(jnp.float32)).astype(x.dtype)


def layernorm(x, g, b):
    xf = x.astype(jnp.float32)
    mu = jnp.mean(xf, axis=-1, keepdims=True)
    var = jnp.mean(jnp.square(xf - mu), axis=-1, keepdims=True)
    y = (xf - mu) * lax.rsqrt(var + EPS) * g.astype(jnp.float32) + b.astype(jnp.float32)
    return y.astype(x.dtype)


def modulate(h, shift, scale):
    return h * (1 + scale[:, None, :]) + shift[:, None, :]


def swiglu(h, w_in, w_out):
    a, b = jnp.split(h @ w_in, 2, axis=-1)
    return (jax.nn.silu(a) * b) @ w_out


def causal_conv(x, buf, w, b):
    T = x.shape[1]
    xp = jnp.concatenate([buf.astype(x.dtype), x], axis=1)
    y = b
    for j in range(CONV_W):
        y = y + xp[:, j:j + T] * w[j]
    return y, xp[:, T:]


def gmlp_spatial(u, v, ws, bs):
    B, T, _ = v.shape
    L = min(T, GMLP_CHUNK)
    NC = T // L
    w = jnp.tril(ws[:, :L, :L])
    vg = v.reshape(B, NC, L, GMLP_GROUPS, GMLP_GROUP_DIM)
    bias = jnp.swapaxes(bs[:, :L], 0, 1)[None, None, :, :, None]
    mixed = jnp.einsum('gts,bcsgd->bctgd', w, vg) + bias
    return u * mixed.reshape(B, T, GMLP_W)


def mlstm_chunked(q, k, v, i_pre, f_pre, C0, n0, m0):
    B, T, H, DH = q.shape
    L = math.gcd(T, MLSTM_CHUNK)
    NC = T // L

    def to_chunks(a):
        return jnp.moveaxis(a.reshape(B, NC, L, *a.shape[2:]), 1, 0)

    xs = (to_chunks(q), to_chunks(k), to_chunks(v), to_chunks(i_pre),
          to_chunks(jax.nn.log_sigmoid(f_pre)))
    causal = jnp.tril(jnp.ones((L, L), dtype=bool))

    def step(carry, inp):
        C, n, m = carry
        qb, kb, vb, ib, lfb = inp
        b = jnp.swapaxes(jnp.cumsum(lfb, axis=1), 1, 2)
        ih = jnp.swapaxes(ib, 1, 2)
        inter = b + m[..., None]
        Dm = b[..., :, None] - b[..., None, :] + ih[..., None, :]
        Dm = jnp.where(causal, Dm, -jnp.inf)
        m_t = jnp.maximum(inter, jnp.max(Dm, axis=-1))
        w_inter = jnp.exp(inter - m_t)
        s = jnp.einsum('blhd,bshd->bhls', qb, kb) * jnp.exp(Dm - m_t[..., None])
        num = (w_inter[..., None] * jnp.einsum('blhd,bhde->bhle', qb, C)
               + jnp.einsum('bhls,bshe->bhle', s, vb))
        den = w_inter * jnp.einsum('blhd,bhd->bhl', qb, n) + jnp.sum(s, axis=-1)
        h = num / jnp.maximum(jnp.abs(den), jnp.exp(-m_t))[..., None]
        bL = b[..., -1]
        g = bL[..., None] - b + ih
        m_new = jnp.maximum(bL + m, jnp.max(g, axis=-1))
        a_prev = jnp.exp(bL + m - m_new)
        kw = kb * jnp.swapaxes(jnp.exp(g - m_new[..., None]), 1, 2)[..., None]
        C_new = a_prev[..., None, None] * C + jnp.einsum('bshd,bshe->bhde', kw, vb)
        n_new = a_prev[..., None] * n + jnp.sum(kw, axis=1)
        return (C_new, n_new, m_new), jnp.swapaxes(h, 1, 2)

    (C, n, m), hs = lax.scan(step, (C0, n0, m0), xs)
    return jnp.moveaxis(hs, 0, 1).reshape(B, T, H, DH), C, n, m


def token_mixer(h, conv_buf, C0, n0, m0, w_in, b_gates, conv_w, conv_b, ln_g, ln_b,
                ws, bs, norm_g, w_out):
    B, T, _ = h.shape
    H, DH = MLSTM_HEADS, MLSTM_HEAD_DIM
    z = h @ w_in
    u = jax.nn.gelu(z[..., :GMLP_W])
    v = layernorm(jax.nn.gelu(z[..., GMLP_W:2 * GMLP_W]), ln_g, ln_b)
    y_a = gmlp_spatial(u, v, ws, bs)
    s0 = 2 * GMLP_W
    qk, new_buf = causal_conv(z[..., s0:s0 + 2 * MLSTM_W], conv_buf, conv_w, conv_b)
    qk = jax.nn.silu(qk)
    s0 += 2 * MLSTM_W
    vm = z[..., s0:s0 + MLSTM_W]
    s0 += MLSTM_W
    o = z[..., s0:s0 + MLSTM_W]
    s0 += MLSTM_W
    gif = (z[..., s0:s0 + 2 * H] + b_gates).astype(jnp.float32)
    s0 += 2 * H
    g_a = z[..., s0:s0 + D_MODEL]
    g_b = z[..., s0 + D_MODEL:s0 + 2 * D_MODEL]

    def heads(a):
        return a.astype(jnp.float32).reshape(B, T, H, DH)

    q = heads(qk[..., :MLSTM_W])
    k = heads(qk[..., MLSTM_W:]) * (DH ** -0.5)
    ht, C, n, m = mlstm_chunked(q, k, heads(vm), gif[..., :H], gif[..., H:],
                                C0.astype(jnp.float32), n0.astype(jnp.float32),
                                m0.astype(jnp.float32))
    mu = jnp.mean(ht, axis=-1, keepdims=True)
    var = jnp.mean(jnp.square(ht - mu), axis=-1, keepdims=True)
    hn = (ht - mu) * lax.rsqrt(var + EPS) * norm_g.astype(jnp.float32).reshape(H, DH)
    y_b = jax.nn.sigmoid(o) * hn.reshape(B, T, MLSTM_W).astype(h.dtype)
    mix = jax.nn.sigmoid(g_a) * y_a + jax.nn.sigmoid(g_b) * y_b
    return mix @ w_out, new_buf, C, n, m, v


def run_trunk(x, c, conv0, C0, n0, m0, w_ada, b_ada, g_ffn1, w_ffn1_in, w_ffn1_out, g_mix,
              w_in, b_gates, conv_w, conv_b, gmlp_ln_g, gmlp_ln_b, gmlp_ws, gmlp_bs,
              mlstm_norm_g, w_out, g_ffn2, w_ffn2_in, w_ffn2_out, w_ada_final, b_ada_final,
              g_final):
    B = x.shape[0]
    cs = jax.nn.silu(c)
    outs = []
    for l in range(DEPTH):
        ada = (cs @ w_ada[l] + b_ada[l]).reshape(B, N_ADA, D_MODEL)
        sh1, sc1, gt1, sh2, sc2, gt2, sh3, sc3, gt3 = [ada[:, j] for j in range(N_ADA)]
        h = modulate(rmsnorm(x, g_ffn1[l]), sh1, sc1)
        x = x + 0.5 * gt1[:, None] * swiglu(h, w_ffn1_in[l], w_ffn1_out[l])
        h = modulate(rmsnorm(x, g_mix[l]), sh2, sc2)
        out, buf, C, n, m, v = token_mixer(h, conv0[l], C0[l], n0[l], m0[l], w_in[l], b_gates[l],
                                           conv_w[l], conv_b[l], gmlp_ln_g[l], gmlp_ln_b[l],
                                           gmlp_ws[l], gmlp_bs[l], mlstm_norm_g[l], w_out[l])
        x = x + gt2[:, None] * out
        h = modulate(rmsnorm(x, g_ffn2[l]), sh3, sc3)
        x = x + 0.5 * gt3[:, None] * swiglu(h, w_ffn2_in[l], w_ffn2_out[l])
        outs.append((buf, C.astype(x.dtype), n.astype(x.dtype), m.astype(x.dtype), v))
    fin = cs @ w_ada_final + b_ada_final
    shf, scf = jnp.split(fin, 2, axis=-1)
    y = modulate(rmsnorm(x, g_final), shf, scf)
    buf_s = jnp.stack([o[0] for o in outs])
    C_s = jnp.stack([o[1] for o in outs])
    n_s = jnp.stack([o[2] for o in outs])
    m_s = jnp.stack([o[3] for o in outs])
    v_s = jnp.stack([o[4] for o in outs])
    return y, buf_s, C_s, n_s, m_s, v_s


def setup_inputs(seed: int = 0) -> dict:
    key = jax.random.key(seed)
    ks = iter(jax.random.split(key, 48))

    def nrm(shape, s):
        return s * jax.random.normal(next(ks), shape, jnp.float32)

    D, H, DH = D_MODEL, MLSTM_HEADS, MLSTM_HEAD_DIM
    f_bias = jnp.linspace(3.0, 6.0, H, dtype=jnp.float32)
    b_gates = jnp.concatenate([nrm((DEPTH, H), 0.1), f_bias[None] + nrm((DEPTH, H), 0.1)], axis=-1)
    return {
        'x_prompt': nrm((BATCH, SEQ, D), 1.0),
        'x_sample': nrm((DEC_BATCH, DEC_SEQ, D), 1.0),
        'c_prompt': nrm((BATCH, D), 1.0),
        'c_sample': nrm((DEC_BATCH, D), 1.0),
        'state_mlstm_C': nrm((DEPTH, DEC_BATCH, H, DH, DH), 0.1),
        'state_mlstm_n': nrm((DEPTH, DEC_BATCH, H, DH), 1.0),
        'state_mlstm_m': nrm((DEPTH, DEC_BATCH, H), 0.5),
        'state_conv': nrm((DEPTH, DEC_BATCH, CONV_W - 1, 2 * MLSTM_W), 1.0),
        'w_ada': nrm((DEPTH, D, N_ADA * D), 0.5 * D ** -0.5),
        'b_ada': nrm((DEPTH, N_ADA * D), 0.1),
        'g_ffn1': 1.0 + nrm((DEPTH, D), 0.1),
        'w_ffn1_in': nrm((DEPTH, D, 2 * D_FF), D ** -0.5),
        'w_ffn1_out': nrm((DEPTH, D_FF, D), D_FF ** -0.5),
        'g_mix': 1.0 + nrm((DEPTH, D), 0.1),
        'w_in': nrm((DEPTH, D, IN_COLS), D ** -0.5),
        'b_gates': b_gates,
        'conv_w': nrm((DEPTH, CONV_W, 2 * MLSTM_W), CONV_W ** -0.5),
        'conv_b': nrm((DEPTH, 2 * MLSTM_W), 0.02),
        'gmlp_ln_g': 1.0 + nrm((DEPTH, GMLP_W), 0.1),
        'gmlp_ln_b': nrm((DEPTH, GMLP_W), 0.02),
        'gmlp_ws': nrm((DEPTH, GMLP_GROUPS, GMLP_CHUNK, GMLP_CHUNK), 0.05),
        'gmlp_bs': 1.0 + nrm((DEPTH, GMLP_GROUPS, GMLP_CHUNK), 0.1),
        'mlstm_norm_g': 1.0 + nrm((DEPTH, MLSTM_W), 0.1),
        'w_out': nrm((DEPTH, D, D), D ** -0.5),
        'g_ffn2': 1.0 + nrm((DEPTH, D), 0.1),
        'w_ffn2_in': nrm((DEPTH, D, 2 * D_FF), D ** -0.5),
        'w_ffn2_out': nrm((DEPTH, D_FF, D), D_FF ** -0.5),
        'w_ada_final': nrm((D, 2 * D), 0.5 * D ** -0.5),
        'b_ada_final': nrm((2 * D,), 0.1),
        'g_final': 1.0 + nrm((D,), 0.1),
    }


def reference(x_prompt, x_sample, c_prompt, c_sample, state_mlstm_C, state_mlstm_n,
              state_mlstm_m, state_conv, w_ada, b_ada, g_ffn1, w_ffn1_in, w_ffn1_out, g_mix,
              w_in, b_gates, conv_w, conv_b, gmlp_ln_g, gmlp_ln_b, gmlp_ws, gmlp_bs,
              mlstm_norm_g, w_out, g_ffn2, w_ffn2_in, w_ffn2_out, w_ada_final, b_ada_final,
              g_final):
    weights = (w_ada, b_ada, g_ffn1, w_ffn1_in, w_ffn1_out, g_mix, w_in, b_gates, conv_w,
               conv_b, gmlp_ln_g, gmlp_ln_b, gmlp_ws, gmlp_bs, mlstm_norm_g, w_out, g_ffn2,
               w_ffn2_in, w_ffn2_out, w_ada_final, b_ada_final, g_final)
    Bp = x_prompt.shape[0]
    H, DH = MLSTM_HEADS, MLSTM_HEAD_DIM
    conv0 = jnp.zeros((DEPTH, Bp, CONV_W - 1, 2 * MLSTM_W), x_prompt.dtype)
    C0 = jnp.zeros((DEPTH, Bp, H, DH, DH), jnp.float32)
    n0 = jnp.zeros((DEPTH, Bp, H, DH), jnp.float32)
    m0 = jnp.zeros((DEPTH, Bp, H), jnp.float32)
    y_prompt, conv_p, C_p, n_p, m_p, _ = run_trunk(x_prompt, c_prompt, conv0, C0, n0, m0,
                                                   *weights)
    y_sample, conv_s, C_s, n_s, m_s, v_s = run_trunk(x_sample, c_sample, state_conv,
                                                     state_mlstm_C, state_mlstm_n,
                                                     state_mlstm_m, *weights)
    return (y_prompt, y_sample, C_p, n_p, m_p, conv_p, C_s, n_s, m_s, conv_s, v_s)
```

```python
import functools
import math

import jax
import jax.numpy as jnp
from jax import lax
from jax.experimental import pallas as pl
from jax.experimental.pallas import tpu as pltpu

D_MODEL = 1024
N_ADA = 9
GMLP_GROUPS = 4
GMLP_GROUP_DIM = D_MODEL // GMLP_GROUPS
GMLP_CHUNK = 128
MLSTM_HEADS = 4
MLSTM_HEAD_DIM = D_MODEL // MLSTM_HEADS
MLSTM_CHUNK = 128
CONV_W = 4
EPS = 1e-6
GATE_PAD = 128
HIST_ROWS = 8
VMEM_LIMIT_BYTES = 56 * 1024 * 1024

_BF16 = jnp.bfloat16
_F32 = jnp.float32


def _dot(a, b):
    return jnp.dot(a, b, preferred_element_type=_F32)


def _dot_nt(a, b):
    return lax.dot_general(a, b, (((1,), (1,)), ((), ())), preferred_element_type=_F32)


def _dot_tn(a, b):
    return lax.dot_general(a, b, (((0,), (0,)), ((), ())), preferred_element_type=_F32)


def _sigmoid(x):
    return 1.0 / (1.0 + jnp.exp(-x))


def _silu(x):
    return x * _sigmoid(x)


def _gelu_tanh(x):
    return 0.5 * x * (1.0 + jnp.tanh(0.7978845608028654 * (x + 0.044715 * (x * x * x))))


def _log_sigmoid(x):
    return jnp.minimum(x, 0.0) - jnp.log1p(jnp.exp(-jnp.abs(x)))


def _rms_mod(x, g, shift, scale):
    y = x * lax.rsqrt(jnp.mean(x * x, axis=-1, keepdims=True) + EPS) * g
    return y * (1.0 + scale) + shift


def _resident(shape):
    nd = len(shape)
    return pl.BlockSpec(shape, lambda *_: (0,) * nd, pipeline_mode=pl.Buffered(1))


def _params(semantics):
    return pltpu.CompilerParams(dimension_semantics=semantics, vmem_limit_bytes=VMEM_LIMIT_BYTES)


def _ada_kernel(c_ref, w_ref, b_ref, o_ref):
    cs = _silu(c_ref[...]).astype(_BF16)
    o_ref[...] = _dot(cs, w_ref[...].astype(_BF16)) + b_ref[...]


def _ada(c, w, b, bn):
    bc, d = c.shape
    n = w.shape[1]
    return pl.pallas_call(
        _ada_kernel,
        out_shape=jax.ShapeDtypeStruct((bc, n), _F32),
        grid=(n // bn,),
        in_specs=[pl.BlockSpec((bc, d), lambda j: (0, 0)),
                  pl.BlockSpec((d, bn), lambda j: (0, j)),
                  pl.BlockSpec((1, bn), lambda j: (0, j))],
        out_specs=pl.BlockSpec((bc, bn), lambda j: (0, j)),
        compiler_params=_params(("arbitrary",)),
        name="ada",
    )(c, w, b.reshape(1, n))


def _ffn_kernel(x_ref, ada_ref, g_ref, wa_ref, wb_ref, wo_ref, *rest, j, final):
    if final:
        fin_ref, gf_ref, o_ref = rest
    else:
        (o_ref,) = rest
    bb, tt, d = x_ref.shape
    x = x_ref[...]
    h = _rms_mod(x, g_ref[...], ada_ref[:, j:j + 1, :], ada_ref[:, j + 1:j + 2, :])
    hb = h.reshape(bb * tt, d).astype(_BF16)
    a = _dot(hb, wa_ref[...])
    b = _dot(hb, wb_ref[...])
    gated = (_silu(a) * b).astype(_BF16)
    out = _dot(gated, wo_ref[...]).reshape(bb, tt, d)
    y = x + 0.5 * ada_ref[:, j + 2:j + 3, :] * out
    if final:
        y = _rms_mod(y, gf_ref[...], fin_ref[:, 0:1, :], fin_ref[:, 1:2, :])
    o_ref[...] = y


def _ffn(x, ada, g, wa, wb, wo, *, j, bb, tt, fin=None, g_final=None):
    B, T, d = x.shape
    dff = wa.shape[1]
    final = fin is not None
    tok = lambda b, t: (b, t, 0)
    row = lambda b, t: (b, 0, 0)
    in_specs = [pl.BlockSpec((bb, tt, d), tok),
                pl.BlockSpec((bb, N_ADA, d), row),
                _resident((1, d)), _resident((d, dff)), _resident((d, dff)), _resident((dff, d))]
    args = [x, ada, g.reshape(1, d), wa, wb, wo]
    if final:
        in_specs += [pl.BlockSpec((bb, 2, d), row), _resident((1, d))]
        args += [fin, g_final.reshape(1, d)]
    return pl.pallas_call(
        functools.partial(_ffn_kernel, j=j, final=final),
        out_shape=jax.ShapeDtypeStruct((B, T, d), _F32),
        grid=(B // bb, T // tt),
        in_specs=in_specs,
        out_specs=pl.BlockSpec((bb, tt, d), tok),
        compiler_params=_params(("arbitrary", "arbitrary")),
        name="ffn_final" if final else "ffn",
    )(*args)


def _mix_in_kernel(x_ref, ada_ref, g_ref, wu_ref, wv_ref, wqk_ref, wvm_ref, wo_ref, wif_ref,
                   wga_ref, wgb_ref, bg_ref, cw_ref, cb_ref, lng_ref, lnb_ref, wt_ref, bias_ref,
                   *rest, L, zero_hist, emit_v):
    rest = list(rest)
    conv0_ref = None if zero_hist else rest.pop(0)
    ya_ref, gb_ref, q_ref, k_ref, vm_ref, gcol_ref, conv_ref = rest[:7]
    rest = rest[7:]
    v_ref = rest.pop(0) if emit_v else None
    (xp_ref,) = rest

    bb, tt, d = x_ref.shape
    M = bb * tt
    t_idx = pl.program_id(1)

    h = _rms_mod(x_ref[...], g_ref[...], ada_ref[:, 3:4, :], ada_ref[:, 4:5, :])
    hb = h.reshape(M, d).astype(_BF16)

    u = _gelu_tanh(_dot(hb, wu_ref[...]))
    gv = _gelu_tanh(_dot(hb, wv_ref[...]))
    mu = jnp.mean(gv, axis=-1, keepdims=True)
    var = jnp.mean(jnp.square(gv - mu), axis=-1, keepdims=True)
    v = (gv - mu) * lax.rsqrt(var + EPS) * lng_ref[...] + lnb_ref[...]
    if emit_v:
        v_ref[...] = v.reshape(bb, tt, d)
    vb = v.astype(_BF16)
    rows = lax.broadcasted_iota(jnp.int32, (M, M), 0)
    cols = lax.broadcasted_iota(jnp.int32, (M, M), 1)
    mask = jnp.logical_and(rows // L == cols // L, cols <= rows)
    sig_ga = _sigmoid(_dot(hb, wga_ref[...]))
    for g in range(GMLP_GROUPS):
        sl = slice(g * GMLP_GROUP_DIM, (g + 1) * GMLP_GROUP_DIM)
        wg = jnp.where(mask, wt_ref[g], 0.0).astype(_BF16)
        mixed = _dot(wg, vb[:, sl]) + bias_ref[:, g:g + 1]
        ya_ref[:, :, sl] = (sig_ga[:, sl] * u[:, sl] * mixed).reshape(bb, tt, GMLP_GROUP_DIM)

    zqk = _dot(hb, wqk_ref[...])
    nqk = zqk.shape[-1]

    @pl.when(t_idx == 0)
    def _():
        if zero_hist:
            xp_ref[:, 0:HIST_ROWS, :] = jnp.zeros((bb, HIST_ROWS, nqk), _F32)
        else:
            xp_ref[:, HIST_ROWS - (CONV_W - 1):HIST_ROWS, :] = conv0_ref[...]

    xp_ref[:, HIST_ROWS:HIST_ROWS + tt, :] = zqk.reshape(bb, tt, nqk)
    conv = cb_ref[...]
    for jj in range(CONV_W):
        lo = HIST_ROWS - (CONV_W - 1) + jj
        conv = conv + xp_ref[:, lo:lo + tt, :] * cw_ref[jj:jj + 1, :]
    hist = xp_ref[:, tt + HIST_ROWS - (CONV_W - 1):tt + HIST_ROWS, :]
    xp_ref[:, HIST_ROWS - (CONV_W - 1):HIST_ROWS, :] = hist
    conv_ref[...] = hist
    qk = _silu(conv)
    q_ref[...] = qk[:, :, :d]
    k_ref[...] = qk[:, :, d:]
    vm_ref[...] = _dot(hb, wvm_ref[...]).reshape(bb, tt, d)
    gate_b = _sigmoid(_dot(hb, wo_ref[...])) * _sigmoid(_dot(hb, wgb_ref[...]))
    gb_ref[...] = gate_b.reshape(bb, tt, d)
    gif = _dot(hb, wif_ref[...])[:, :2 * MLSTM_HEADS] + bg_ref[...]
    gcol_ref[...] = gif.reshape(bb, tt, 2 * MLSTM_HEADS)


def _mix_in(x, ada, g_mix, wts, b_gates, conv_w, conv_b, ln_g, ln_b, wtile, bias_big, conv0,
            *, bb, tt, L, emit_v):
    B, T, d = x.shape
    wu, wv, wqk, wvm, wo, wif, wga, wgb = wts
    nqk = wqk.shape[1]
    M = bb * tt
    zero_hist = conv0 is None
    tok = lambda b, t: (b, t, 0)
    row = lambda b, t: (b, 0, 0)
    in_specs = [pl.BlockSpec((bb, tt, d), tok), pl.BlockSpec((bb, N_ADA, d), row), _resident((1, d))]
    in_specs += [_resident(w.shape) for w in wts]
    in_specs += [_resident((1, 2 * MLSTM_HEADS)), _resident((CONV_W, nqk)), _resident((1, nqk)),
                 _resident((1, d)), _resident((1, d)), _resident((GMLP_GROUPS, M, M)),
                 _resident((M, GMLP_GROUPS))]
    args = [x, ada, g_mix.reshape(1, d), *wts, b_gates.reshape(1, -1), conv_w, conv_b.reshape(1, nqk),
            ln_g.reshape(1, d), ln_b.reshape(1, d), wtile, bias_big]
    if not zero_hist:
        in_specs.append(pl.BlockSpec((bb, CONV_W - 1, nqk), row))
        args.append(conv0)
    tok_out = jax.ShapeDtypeStruct((B, T, d), _F32)
    out_shape = [tok_out] * 5 + [jax.ShapeDtypeStruct((B, T, 2 * MLSTM_HEADS), _F32),
                                 jax.ShapeDtypeStruct((B, CONV_W - 1, nqk), _F32)]
    out_specs = [pl.BlockSpec((bb, tt, d), tok)] * 5 + [
        pl.BlockSpec((bb, tt, 2 * MLSTM_HEADS), tok), pl.BlockSpec((bb, CONV_W - 1, nqk), row)]
    if emit_v:
        out_shape.append(tok_out)
        out_specs.append(pl.BlockSpec((bb, tt, d), tok))
    return pl.pallas_call(
        functools.partial(_mix_in_kernel, L=L, zero_hist=zero_hist, emit_v=emit_v),
        out_shape=out_shape,
        grid=(B // bb, T // tt),
        in_specs=in_specs,
        out_specs=out_specs,
        scratch_shapes=[pltpu.VMEM((bb, tt + HIST_ROWS, nqk), _F32)],
        compiler_params=_params(("arbitrary", "arbitrary")),
        name="mix_in",
    )(*args)


def _mlstm_kernel(q_ref, k_ref, v_ref, gcol_ref, ng_ref, *rest, zero_state):
    if zero_state:
        hn_ref, C_ref, n_ref, m_ref = rest
    else:
        C0_ref, n0_ref, m0_ref, hn_ref, C_ref, n_ref, m_ref = rest
    L = q_ref.shape[1]
    DH = MLSTM_HEAD_DIM

    @pl.when(pl.program_id(1) == 0)
    def _():
        if zero_state:
            C_ref[...] = jnp.zeros(C_ref.shape, _F32)
            n_ref[...] = jnp.zeros(n_ref.shape, _F32)
            m_ref[...] = jnp.zeros(m_ref.shape, _F32)
        else:
            C_ref[...] = C0_ref[...]
            n_ref[...] = n0_ref[...]
            m_ref[...] = m0_ref[...]

    ti = lax.broadcasted_iota(jnp.int32, (L, L), 0)
    si = lax.broadcasted_iota(jnp.int32, (L, L), 1)
    eye = ti == si
    causal = si <= ti
    gates = gcol_ref[0]

    for hd in range(MLSTM_HEADS):
        sl = slice(hd * DH, (hd + 1) * DH)
        qh = q_ref[0, :, sl]
        kh = k_ref[0, :, sl] * (DH ** -0.5)
        qb = qh.astype(_BF16)
        kb = kh.astype(_BF16)
        vb = v_ref[0, :, sl].astype(_BF16)
        Ch = C_ref[0, hd]
        nh = n_ref[0, hd:hd + 1, :]
        m_old = m_ref[0, hd:hd + 1, 0:1]

        i_col = gates[:, hd:hd + 1]
        lf_col = _log_sigmoid(gates[:, MLSTM_HEADS + hd:MLSTM_HEADS + hd + 1])
        lf_row = jnp.sum(jnp.where(eye, lf_col, 0.0), axis=0, keepdims=True)
        b_col = jnp.sum(jnp.where(causal, lf_row, 0.0), axis=1, keepdims=True)
        c_col = i_col - b_col
        r_row = jnp.sum(jnp.where(eye, c_col, 0.0), axis=0, keepdims=True)
        b_last = b_col[L - 1:L, :]

        inter = b_col + m_old
        dm = jnp.where(causal, b_col + r_row, -jnp.inf)
        m_t = jnp.maximum(inter, jnp.max(dm, axis=1, keepdims=True))
        w_inter = jnp.exp(inter - m_t)
        s = _dot_nt(qb, kb) * jnp.exp(dm - m_t)
        num = w_inter * _dot(qb, Ch.astype(_BF16)) + _dot(s.astype(_BF16), vb)
        den = (w_inter * jnp.sum(qh * nh, axis=1, keepdims=True)
               + jnp.sum(s, axis=1, keepdims=True))
        hh = num * (1.0 / jnp.maximum(jnp.abs(den), jnp.exp(-m_t)))

        m_new = jnp.maximum(b_last + m_old, jnp.max(b_last + r_row, axis=1, keepdims=True))
        a_prev = jnp.exp(b_last + m_old - m_new)
        kw = kh * jnp.exp(b_last + c_col - m_new)
        C_ref[0, hd] = a_prev * Ch + _dot_tn(kw.astype(_BF16), vb)
        n_ref[0, hd:hd + 1, :] = a_prev * nh + jnp.sum(kw, axis=0, keepdims=True)
        m_ref[0, hd:hd + 1, :] = jnp.broadcast_to(m_new, (1, m_ref.shape[2]))

        mu = jnp.mean(hh, axis=1, keepdims=True)
        var = jnp.mean(jnp.square(hh - mu), axis=1, keepdims=True)
        hn_ref[0, :, sl] = (hh - mu) * lax.rsqrt(var + EPS) * ng_ref[:, sl]


def _mlstm(q, k, v, gcol, norm_g, state, *, L):
    B, T, d = q.shape
    H, DH = MLSTM_HEADS, MLSTM_HEAD_DIM
    zero_state = state is None
    tok = lambda b, c: (b, c, 0)
    in_specs = [pl.BlockSpec((1, L, d), tok)] * 3 + [pl.BlockSpec((1, L, 2 * H), tok), _resident((1, d))]
    args = [q, k, v, gcol, norm_g.reshape(1, d)]
    st_specs = [pl.BlockSpec((1, H, DH, DH), lambda b, c: (b, 0, 0, 0)),
                pl.BlockSpec((1, H, DH), lambda b, c: (b, 0, 0)),
                pl.BlockSpec((1, H, GATE_PAD), lambda b, c: (b, 0, 0))]
    if not zero_state:
        in_specs += st_specs
        args += list(state)
    return pl.pallas_call(
        functools.partial(_mlstm_kernel, zero_state=zero_state),
        out_shape=[jax.ShapeDtypeStruct((B, T, d), _F32),
                   jax.ShapeDtypeStruct((B, H, DH, DH), _F32),
                   jax.ShapeDtypeStruct((B, H, DH), _F32),
                   jax.ShapeDtypeStruct((B, H, GATE_PAD), _F32)],
        grid=(B, T // L),
        in_specs=in_specs,
        out_specs=[pl.BlockSpec((1, L, d), tok)] + st_specs,
        compiler_params=_params(("arbitrary", "arbitrary")),
        name="mlstm",
    )(*args)


def _mix_out_kernel(x_ref, ada_ref, ya_ref, gb_ref, hn_ref, w_ref, o_ref):
    bb, tt, d = x_ref.shape
    mix = ya_ref[...] + gb_ref[...] * hn_ref[...]
    out = _dot(mix.reshape(bb * tt, d).astype(_BF16), w_ref[...]).reshape(bb, tt, d)
    o_ref[...] = x_ref[...] + ada_ref[:, 5:6, :] * out


def _mix_out(x, ada, ya, gb, hn, w_out, *, bb, tt):
    B, T, d = x.shape
    tok = lambda b, t: (b, t, 0)
    blk = pl.BlockSpec((bb, tt, d), tok)
    return pl.pallas_call(
        _mix_out_kernel,
        out_shape=jax.ShapeDtypeStruct((B, T, d), _F32),
        grid=(B // bb, T // tt),
        in_specs=[blk, pl.BlockSpec((bb, N_ADA, d), lambda b, t: (b, 0, 0)), blk, blk, blk,
                  _resident((d, d))],
        out_specs=blk,
        compiler_params=_params(("arbitrary", "arbitrary")),
        name="mix_out",
    )(x, ada, ya, gb, hn, w_out)


def _trunk(x, ada, fin, conv0, state, W, *, bb, tt, emit_v):
    B, T, d = x.shape
    L = min(T, GMLP_CHUNK)
    assert L == math.gcd(T, MLSTM_CHUNK) and tt % L == 0 and tt % 8 == 0
    M = bb * tt
    wtile = jnp.tile(W["gmlp_ws"][:, :L, :L], (1, M // L, M // L))
    bias_big = jnp.tile(W["gmlp_bs"][:, :L].T, (M // L, 1))

    x = _ffn(x, ada, W["g_ffn1"], *W["ffn1"], j=0, bb=bb, tt=tt)
    outs = _mix_in(x, ada, W["g_mix"], W["w_in_parts"], W["b_gates"], W["conv_w"], W["conv_b"],
                   W["gmlp_ln_g"], W["gmlp_ln_b"], wtile, bias_big, conv0,
                   bb=bb, tt=tt, L=L, emit_v=emit_v)
    ya, gb, q, k, vm, gcol, conv_new = outs[:7]
    hn, C, n, m = _mlstm(q, k, vm, gcol, W["mlstm_norm_g"], state, L=L)
    x = _mix_out(x, ada, ya, gb, hn, W["w_out"], bb=bb, tt=tt)
    y = _ffn(x, ada, W["g_ffn2"], *W["ffn2"], j=6, bb=bb, tt=tt, fin=fin, g_final=W["g_final"])
    v = outs[7] if emit_v else None
    return y, conv_new[None], C[None], n[None], m[None, :, :, 0], v


def kernel(x_prompt, x_sample, c_prompt, c_sample, state_mlstm_C, state_mlstm_n, state_mlstm_m, state_conv, w_ada, b_ada, g_ffn1, w_ffn1_in, w_ffn1_out, g_mix, w_in, b_gates, conv_w, conv_b, gmlp_ln_g, gmlp_ln_b, gmlp_ws, gmlp_bs, mlstm_norm_g, w_out, g_ffn2, w_ffn2_in, w_ffn2_out, w_ada_final, b_ada_final, g_final):
    assert w_ada.shape[0] == 1, "single-layer trunk only"
    d, H = D_MODEL, MLSTM_HEADS
    Bp = x_prompt.shape[0]
    bf = lambda a: a.astype(_BF16)

    def ffn_w(w_i, w_o):
        dff = w_o.shape[1]
        return bf(w_i[0, :, :dff]), bf(w_i[0, :, dff:]), bf(w_o[0])

    wi = w_in[0]
    s = [0, d, 2 * d, 4 * d, 5 * d, 6 * d, 6 * d + 2 * H, 7 * d + 2 * H, 8 * d + 2 * H]
    wu, wv, wqk, wvm, wo, wif, wga, wgb = [wi[:, s[i]:s[i + 1]] for i in range(8)]
    wif = jnp.pad(wif, ((0, 0), (0, GATE_PAD - 2 * H)))
    W = {
        "ffn1": ffn_w(w_ffn1_in, w_ffn1_out), "ffn2": ffn_w(w_ffn2_in, w_ffn2_out),
        "w_in_parts": tuple(bf(w) for w in (wu, wv, wqk, wvm, wo, wif, wga, wgb)),
        "w_out": bf(w_out[0]),
        "g_ffn1": g_ffn1[0], "g_mix": g_mix[0], "g_ffn2": g_ffn2[0], "g_final": g_final,
        "b_gates": b_gates[0], "conv_w": conv_w[0], "conv_b": conv_b[0],
        "gmlp_ln_g": gmlp_ln_g[0], "gmlp_ln_b": gmlp_ln_b[0],
        "gmlp_ws": gmlp_ws[0], "gmlp_bs": gmlp_bs[0], "mlstm_norm_g": mlstm_norm_g[0],
    }

    c_all = jnp.concatenate([c_prompt, c_sample], axis=0)
    ada_all = _ada(c_all, w_ada[0], b_ada[0], 1024).reshape(-1, N_ADA, d)
    fin_all = _ada(c_all, w_ada_final, b_ada_final, 1024).reshape(-1, 2, d)

    y_p, conv_p, C_p, n_p, m_p, _ = _trunk(
        x_prompt, ada_all[:Bp], fin_all[:Bp], None, None, W, bb=1, tt=256, emit_v=False)
    m0 = jnp.broadcast_to(state_mlstm_m[0][:, :, None], state_mlstm_m.shape[1:] + (GATE_PAD,))
    y_s, conv_s, C_s, n_s, m_s, v_s = _trunk(
        x_sample, ada_all[Bp:], fin_all[Bp:], state_conv[0],
        (state_mlstm_C[0], state_mlstm_n[0], m0), W, bb=32, tt=8, emit_v=True)
    return (y_p, y_s, C_p, n_p, m_p, conv_p, C_s, n_s, m_s, conv_s, v_s[None])
```

```python
import functools
import math

import jax
import jax.numpy as jnp
from jax import lax
from jax.experimental import pallas as pl
from jax.experimental.pallas import tpu as pltpu

D_MODEL = 1024
N_ADA = 9
GMLP_GROUPS = 4
GMLP_GROUP_DIM = D_MODEL // GMLP_GROUPS
GMLP_CHUNK = 128
MLSTM_HEADS = 4
MLSTM_HEAD_DIM = D_MODEL // MLSTM_HEADS
MLSTM_CHUNK = 128
CONV_W = 4
EPS = 1e-6
GATE_PAD = 128
HIST_ROWS = 8
VMEM_LIMIT_BYTES = 56 * 1024 * 1024

_BF16 = jnp.bfloat16
_F32 = jnp.float32


def _dot(a, b):
    return jnp.dot(a, b, preferred_element_type=_F32)


def _dot_nt(a, b):
    return lax.dot_general(a, b, (((1,), (1,)), ((), ())), preferred_element_type=_F32)


def _bdot(a, b, ca, cb):
    return lax.dot_general(a, b, (((ca,), (cb,)), ((0,), (0,))), preferred_element_type=_F32)


def _sigmoid(x):
    return 1.0 / (1.0 + jnp.exp(-x))


def _silu(x):
    return x * _sigmoid(x)


def _gelu_tanh(x):
    return 0.5 * x * (1.0 + jnp.tanh(0.7978845608028654 * (x + 0.044715 * (x * x * x))))


def _log_sigmoid(x):
    return jnp.minimum(x, 0.0) - jnp.log1p(jnp.exp(-jnp.abs(x)))


def _rms_mod(x, g, shift, scale):
    y = x * lax.rsqrt(jnp.mean(x * x, axis=-1, keepdims=True) + EPS) * g
    return y * (1.0 + scale) + shift


def _resident(shape):
    nd = len(shape)
    return pl.BlockSpec(shape, lambda *_: (0,) * nd, pipeline_mode=pl.Buffered(1))


def _params(semantics):
    return pltpu.CompilerParams(dimension_semantics=semantics, vmem_limit_bytes=VMEM_LIMIT_BYTES)


def _ada_kernel(c_ref, w_ref, b_ref, o_ref):
    cs = _silu(c_ref[...]).astype(_BF16)
    o_ref[...] = _dot(cs, w_ref[...].astype(_BF16)) + b_ref[...]


def _ada(c, w, b, bn):
    bc, d = c.shape
    n = w.shape[1]
    return pl.pallas_call(
        _ada_kernel,
        out_shape=jax.ShapeDtypeStruct((bc, n), _F32),
        grid=(n // bn,),
        in_specs=[pl.BlockSpec((bc, d), lambda j: (0, 0)),
                  pl.BlockSpec((d, bn), lambda j: (0, j)),
                  pl.BlockSpec((1, bn), lambda j: (0, j))],
        out_specs=pl.BlockSpec((bc, bn), lambda j: (0, j)),
        compiler_params=_params(("arbitrary",)),
        name="ada",
    )(c, w, b.reshape(1, n))


def _ffn_kernel(x_ref, ada_ref, g_ref, wa_ref, wb_ref, wo_ref, *rest, j, final):
    if final:
        fin_ref, gf_ref, o_ref = rest
    else:
        (o_ref,) = rest
    bb, tt, d = x_ref.shape
    x = x_ref[...]
    h = _rms_mod(x, g_ref[...], ada_ref[:, j:j + 1, :], ada_ref[:, j + 1:j + 2, :])
    hb = h.reshape(bb * tt, d).astype(_BF16)
    a = _dot(hb, wa_ref[...])
    b = _dot(hb, wb_ref[...])
    gated = (_silu(a) * b).astype(_BF16)
    out = _dot(gated, wo_ref[...]).reshape(bb, tt, d)
    y = x + 0.5 * ada_ref[:, j + 2:j + 3, :] * out
    if final:
        y = _rms_mod(y, gf_ref[...], fin_ref[:, 0:1, :], fin_ref[:, 1:2, :])
    o_ref[...] = y


def _ffn(x, ada, g, wa, wb, wo, *, j, bb, tt, fin=None, g_final=None):
    B, T, d = x.shape
    dff = wa.shape[1]
    final = fin is not None
    tok = lambda b, t: (b, t, 0)
    row = lambda b, t: (b, 0, 0)
    in_specs = [pl.BlockSpec((bb, tt, d), tok),
                pl.BlockSpec((bb, N_ADA, d), row),
                _resident((1, d)), _resident((d, dff)), _resident((d, dff)), _resident((dff, d))]
    args = [x, ada, g.reshape(1, d), wa, wb, wo]
    if final:
        in_specs += [pl.BlockSpec((bb, 2, d), row), _resident((1, d))]
        args += [fin, g_final.reshape(1, d)]
    return pl.pallas_call(
        functools.partial(_ffn_kernel, j=j, final=final),
        out_shape=jax.ShapeDtypeStruct((B, T, d), _F32),
        grid=(B // bb, T // tt),
        in_specs=in_specs,
        out_specs=pl.BlockSpec((bb, tt, d), tok),
        compiler_params=_params(("arbitrary", "arbitrary")),
        name="ffn_final" if final else "ffn",
    )(*args)


def _mix_in_kernel(x_ref, ada_ref, g_ref, wu_ref, wv_ref, wqk_ref, wvm_ref, wo_ref, wif_ref,
                   wga_ref, wgb_ref, bg_ref, cw_ref, cb_ref, lng_ref, lnb_ref, wt_ref, bias_ref,
                   *rest, L, zero_hist, emit_v):
    rest = list(rest)
    conv0_ref = None if zero_hist else rest.pop(0)
    ya_ref, gb_ref, q_ref, k_ref, vm_ref, gcol_ref, conv_ref = rest[:7]
    rest = rest[7:]
    v_ref = rest.pop(0) if emit_v else None
    (xp_ref,) = rest

    bb, tt, d = x_ref.shape
    M = bb * tt
    t_idx = pl.program_id(1)

    h = _rms_mod(x_ref[...], g_ref[...], ada_ref[:, 3:4, :], ada_ref[:, 4:5, :])
    hb = h.reshape(M, d).astype(_BF16)

    u = _gelu_tanh(_dot(hb, wu_ref[...]))
    gv = _gelu_tanh(_dot(hb, wv_ref[...]))
    mu = jnp.mean(gv, axis=-1, keepdims=True)
    var = jnp.mean(jnp.square(gv - mu), axis=-1, keepdims=True)
    v = (gv - mu) * lax.rsqrt(var + EPS) * lng_ref[...] + lnb_ref[...]
    if emit_v:
        v_ref[...] = v.reshape(bb, tt, d)
    vb = v.astype(_BF16)
    rows = lax.broadcasted_iota(jnp.int32, (M, M), 0)
    cols = lax.broadcasted_iota(jnp.int32, (M, M), 1)
    mask = jnp.logical_and(rows // L == cols // L, cols <= rows)
    sig_ga = _sigmoid(_dot(hb, wga_ref[...]))
    for g in range(GMLP_GROUPS):
        sl = slice(g * GMLP_GROUP_DIM, (g + 1) * GMLP_GROUP_DIM)
        wg = jnp.where(mask, wt_ref[g], 0.0).astype(_BF16)
        mixed = _dot(wg, vb[:, sl]) + bias_ref[:, g:g + 1]
        ya_ref[:, :, sl] = (sig_ga[:, sl] * u[:, sl] * mixed).reshape(bb, tt, GMLP_GROUP_DIM)

    zqk = _dot(hb, wqk_ref[...])
    nqk = zqk.shape[-1]

    @pl.when(t_idx == 0)
    def _():
        if zero_hist:
            xp_ref[:, 0:HIST_ROWS, :] = jnp.zeros((bb, HIST_ROWS, nqk), _F32)
        else:
            xp_ref[:, HIST_ROWS - (CONV_W - 1):HIST_ROWS, :] = conv0_ref[...]

    xp_ref[:, HIST_ROWS:HIST_ROWS + tt, :] = zqk.reshape(bb, tt, nqk)
    conv = cb_ref[...]
    for jj in range(CONV_W):
        lo = HIST_ROWS - (CONV_W - 1) + jj
        conv = conv + xp_ref[:, lo:lo + tt, :] * cw_ref[jj:jj + 1, :]
    hist = xp_ref[:, tt + HIST_ROWS - (CONV_W - 1):tt + HIST_ROWS, :]
    xp_ref[:, HIST_ROWS - (CONV_W - 1):HIST_ROWS, :] = hist
    conv_ref[...] = hist
    qk = _silu(conv)
    q_ref[...] = qk[:, :, :d]
    k_ref[...] = qk[:, :, d:]
    vm_ref[...] = _dot(hb, wvm_ref[...]).reshape(bb, tt, d)
    gate_b = _sigmoid(_dot(hb, wo_ref[...])) * _sigmoid(_dot(hb, wgb_ref[...]))
    gb_ref[...] = gate_b.reshape(bb, tt, d)
    gif = _dot(hb, wif_ref[...])[:, :2 * MLSTM_HEADS] + bg_ref[...]
    gcol_ref[...] = gif.reshape(bb, tt, 2 * MLSTM_HEADS)


def _mix_in(x, ada, g_mix, wts, b_gates, conv_w, conv_b, ln_g, ln_b, wtile, bias_big, conv0,
            *, bb, tt, L, emit_v):
    B, T, d = x.shape
    wu, wv, wqk, wvm, wo, wif, wga, wgb = wts
    nqk = wqk.shape[1]
    M = bb * tt
    zero_hist = conv0 is None
    tok = lambda b, t: (b, t, 0)
    row = lambda b, t: (b, 0, 0)
    in_specs = [pl.BlockSpec((bb, tt, d), tok), pl.BlockSpec((bb, N_ADA, d), row), _resident((1, d))]
    in_specs += [_resident(w.shape) for w in wts]
    in_specs += [_resident((1, 2 * MLSTM_HEADS)), _resident((CONV_W, nqk)), _resident((1, nqk)),
                 _resident((1, d)), _resident((1, d)), _resident((GMLP_GROUPS, M, M)),
                 _resident((M, GMLP_GROUPS))]
    args = [x, ada, g_mix.reshape(1, d), *wts, b_gates.reshape(1, -1), conv_w, conv_b.reshape(1, nqk),
            ln_g.reshape(1, d), ln_b.reshape(1, d), wtile, bias_big]
    if not zero_hist:
        in_specs.append(pl.BlockSpec((bb, CONV_W - 1, nqk), row))
        args.append(conv0)
    tok_out = jax.ShapeDtypeStruct((B, T, d), _F32)
    out_shape = [tok_out] * 5 + [jax.ShapeDtypeStruct((B, T, 2 * MLSTM_HEADS), _F32),
                                 jax.ShapeDtypeStruct((B, CONV_W - 1, nqk), _F32)]
    out_specs = [pl.BlockSpec((bb, tt, d), tok)] * 5 + [
        pl.BlockSpec((bb, tt, 2 * MLSTM_HEADS), tok), pl.BlockSpec((bb, CONV_W - 1, nqk), row)]
    if emit_v:
        out_shape.append(tok_out)
        out_specs.append(pl.BlockSpec((bb, tt, d), tok))
    return pl.pallas_call(
        functools.partial(_mix_in_kernel, L=L, zero_hist=zero_hist, emit_v=emit_v),
        out_shape=out_shape,
        grid=(B // bb, T // tt),
        in_specs=in_specs,
        out_specs=out_specs,
        scratch_shapes=[pltpu.VMEM((bb, tt + HIST_ROWS, nqk), _F32)],
        compiler_params=_params(("arbitrary", "arbitrary")),
        name="mix_in",
    )(*args)


def _mlstm_kernel(q_ref, k_ref, v_ref, gcol_ref, ng_ref, *rest, zero_state):
    if zero_state:
        hn_ref, C_ref, n_ref, m_ref = rest
    else:
        C0_ref, n0_ref, m0_ref, hn_ref, C_ref, n_ref, m_ref = rest
    rb, L, _ = q_ref.shape
    H, DH = MLSTM_HEADS, MLSTM_HEAD_DIM
    NB = rb * H

    @pl.when(pl.program_id(1) == 0)
    def _():
        if zero_state:
            C_ref[...] = jnp.zeros(C_ref.shape, _F32)
            n_ref[...] = jnp.zeros(n_ref.shape, _F32)
            m_ref[...] = jnp.zeros(m_ref.shape, _F32)
        else:
            C_ref[...] = C0_ref[...]
            n_ref[...] = n0_ref[...]
            m_ref[...] = m0_ref[...]

    ti = lax.broadcasted_iota(jnp.int32, (L, L), 0)
    si = lax.broadcasted_iota(jnp.int32, (L, L), 1)
    eye = ti == si
    causal = si <= ti

    chains = [(r, hd) for r in range(rb) for hd in range(H)]
    heads = lambda ref: jnp.stack([ref[r, :, hd * DH:(hd + 1) * DH] for r, hd in chains])
    qh = heads(q_ref)
    kh = heads(k_ref) * (DH ** -0.5)
    qb = qh.astype(_BF16)
    kb = kh.astype(_BF16)
    vb = heads(v_ref).astype(_BF16)
    Ch = C_ref[...].reshape(NB, DH, DH)
    nh = n_ref[...].reshape(NB, 1, DH)
    m_old = m_ref[...].reshape(NB, 1, GATE_PAD)[:, :, 0:1]
    gates = gcol_ref[...]
    i_col = jnp.stack([gates[r, :, hd:hd + 1] for r, hd in chains])
    lf_col = _log_sigmoid(jnp.stack([gates[r, :, H + hd:H + hd + 1] for r, hd in chains]))

    lf_row = jnp.sum(jnp.where(eye, lf_col, 0.0), axis=1, keepdims=True)
    b_col = jnp.sum(jnp.where(causal, lf_row, 0.0), axis=2, keepdims=True)
    c_col = i_col - b_col
    r_row = jnp.sum(jnp.where(eye, c_col, 0.0), axis=1, keepdims=True)
    b_last = b_col[:, L - 1:L, :]

    inter = b_col + m_old
    dm = jnp.where(causal, b_col + r_row, -jnp.inf)
    m_t = jnp.maximum(inter, jnp.max(dm, axis=2, keepdims=True))
    w_inter = jnp.exp(inter - m_t)
    s = _bdot(qb, kb, 2, 2) * jnp.exp(dm - m_t)
    num = w_inter * _bdot(qb, Ch.astype(_BF16), 2, 1) + _bdot(s.astype(_BF16), vb, 2, 1)
    den = (w_inter * jnp.sum(qh * nh, axis=2, keepdims=True)
           + jnp.sum(s, axis=2, keepdims=True))
    hh = num * (1.0 / jnp.maximum(jnp.abs(den), jnp.exp(-m_t)))

    m_new = jnp.maximum(b_last + m_old, jnp.max(b_last + r_row, axis=2, keepdims=True))
    a_prev = jnp.exp(b_last + m_old - m_new)
    kw = kh * jnp.exp(b_last + c_col - m_new)
    C_new = a_prev * Ch + _bdot(kw.astype(_BF16), vb, 1, 1)
    C_ref[...] = C_new.reshape(rb, H, DH, DH)
    n_ref[...] = (a_prev * nh + jnp.sum(kw, axis=1, keepdims=True)).reshape(rb, H, DH)
    m_ref[...] = jnp.broadcast_to(m_new, (NB, 1, GATE_PAD)).reshape(rb, H, GATE_PAD)

    mu = jnp.mean(hh, axis=2, keepdims=True)
    var = jnp.mean(jnp.square(hh - mu), axis=2, keepdims=True)
    ng = jnp.stack([ng_ref[:, hd * DH:(hd + 1) * DH] for _, hd in chains])
    hn = (hh - mu) * lax.rsqrt(var + EPS) * ng
    for n, (r, hd) in enumerate(chains):
        hn_ref[r, :, hd * DH:(hd + 1) * DH] = hn[n]


def _mlstm(q, k, v, gcol, norm_g, state, *, L, rb):
    B, T, d = q.shape
    H, DH = MLSTM_HEADS, MLSTM_HEAD_DIM
    zero_state = state is None
    tok = lambda b, c: (b, c, 0)
    in_specs = [pl.BlockSpec((rb, L, d), tok)] * 3 + [pl.BlockSpec((rb, L, 2 * H), tok), _resident((1, d))]
    args = [q, k, v, gcol, norm_g.reshape(1, d)]
    st_specs = [pl.BlockSpec((rb, H, DH, DH), lambda b, c: (b, 0, 0, 0)),
                pl.BlockSpec((rb, H, DH), lambda b, c: (b, 0, 0)),
                pl.BlockSpec((rb, H, GATE_PAD), lambda b, c: (b, 0, 0))]
    if not zero_state:
        in_specs += st_specs
        args += list(state)
    return pl.pallas_call(
        functools.partial(_mlstm_kernel, zero_state=zero_state),
        out_shape=[jax.ShapeDtypeStruct((B, T, d), _F32),
                   jax.ShapeDtypeStruct((B, H, DH, DH), _F32),
                   jax.ShapeDtypeStruct((B, H, DH), _F32),
                   jax.ShapeDtypeStruct((B, H, GATE_PAD), _F32)],
        grid=(B // rb, T // L),
        in_specs=in_specs,
        out_specs=[pl.BlockSpec((rb, L, d), tok)] + st_specs,
        compiler_params=_params(("arbitrary", "arbitrary")),
        name="mlstm",
    )(*args)


def _mix_out_kernel(x_ref, ada_ref, ya_ref, gb_ref, hn_ref, w_ref, o_ref):
    bb, tt, d = x_ref.shape
    mix = ya_ref[...] + gb_ref[...] * hn_ref[...]
    out = _dot(mix.reshape(bb * tt, d).astype(_BF16), w_ref[...]).reshape(bb, tt, d)
    o_ref[...] = x_ref[...] + ada_ref[:, 5:6, :] * out


def _mix_out(x, ada, ya, gb, hn, w_out, *, bb, tt):
    B, T, d = x.shape
    tok = lambda b, t: (b, t, 0)
    blk = pl.BlockSpec((bb, tt, d), tok)
    return pl.pallas_call(
        _mix_out_kernel,
        out_shape=jax.ShapeDtypeStruct((B, T, d), _F32),
        grid=(B // bb, T // tt),
        in_specs=[blk, pl.BlockSpec((bb, N_ADA, d), lambda b, t: (b, 0, 0)), blk, blk, blk,
                  _resident((d, d))],
        out_specs=blk,
        compiler_params=_params(("arbitrary", "arbitrary")),
        name="mix_out",
    )(x, ada, ya, gb, hn, w_out)


def _trunk(x, ada, fin, conv0, state, W, *, bb, tt, rb, emit_v):
    B, T, d = x.shape
    L = min(T, GMLP_CHUNK)
    assert L == math.gcd(T, MLSTM_CHUNK) and tt % L == 0 and tt % 8 == 0
    M = bb * tt
    wtile = jnp.tile(W["gmlp_ws"][:, :L, :L], (1, M // L, M // L))
    bias_big = jnp.tile(W["gmlp_bs"][:, :L].T, (M // L, 1))

    x = _ffn(x, ada, W["g_ffn1"], *W["ffn1"], j=0, bb=bb, tt=tt)
    outs = _mix_in(x, ada, W["g_mix"], W["w_in_parts"], W["b_gates"], W["conv_w"], W["conv_b"],
                   W["gmlp_ln_g"], W["gmlp_ln_b"], wtile, bias_big, conv0,
                   bb=bb, tt=tt, L=L, emit_v=emit_v)
    ya, gb, q, k, vm, gcol, conv_new = outs[:7]
    hn, C, n, m = _mlstm(q, k, vm, gcol, W["mlstm_norm_g"], state, L=L, rb=rb)
    x = _mix_out(x, ada, ya, gb, hn, W["w_out"], bb=bb, tt=tt)
    y = _ffn(x, ada, W["g_ffn2"], *W["ffn2"], j=6, bb=bb, tt=tt, fin=fin, g_final=W["g_final"])
    v = outs[7] if emit_v else None
    return y, conv_new[None], C[None], n[None], m[None, :, :, 0], v


def kernel(x_prompt, x_sample, c_prompt, c_sample, state_mlstm_C, state_mlstm_n, state_mlstm_m, state_conv, w_ada, b_ada, g_ffn1, w_ffn1_in, w_ffn1_out, g_mix, w_in, b_gates, conv_w, conv_b, gmlp_ln_g, gmlp_ln_b, gmlp_ws, gmlp_bs, mlstm_norm_g, w_out, g_ffn2, w_ffn2_in, w_ffn2_out, w_ada_final, b_ada_final, g_final):
    assert w_ada.shape[0] == 1, "single-layer trunk only"
    d, H = D_MODEL, MLSTM_HEADS
    Bp = x_prompt.shape[0]
    bf = lambda a: a.astype(_BF16)

    def ffn_w(w_i, w_o):
        dff = w_o.shape[1]
        return bf(w_i[0, :, :dff]), bf(w_i[0, :, dff:]), bf(w_o[0])

    wi = w_in[0]
    s = [0, d, 2 * d, 4 * d, 5 * d, 6 * d, 6 * d + 2 * H, 7 * d + 2 * H, 8 * d + 2 * H]
    wu, wv, wqk, wvm, wo, wif, wga, wgb = [wi[:, s[i]:s[i + 1]] for i in range(8)]
    wif = jnp.pad(wif, ((0, 0), (0, GATE_PAD - 2 * H)))
    W = {
        "ffn1": ffn_w(w_ffn1_in, w_ffn1_out), "ffn2": ffn_w(w_ffn2_in, w_ffn2_out),
        "w_in_parts": tuple(bf(w) for w in (wu, wv, wqk, wvm, wo, wif, wga, wgb)),
        "w_out": bf(w_out[0]),
        "g_ffn1": g_ffn1[0], "g_mix": g_mix[0], "g_ffn2": g_ffn2[0], "g_final": g_final,
        "b_gates": b_gates[0], "conv_w": conv_w[0], "conv_b": conv_b[0],
        "gmlp_ln_g": gmlp_ln_g[0], "gmlp_ln_b": gmlp_ln_b[0],
        "gmlp_ws": gmlp_ws[0], "gmlp_bs": gmlp_bs[0], "mlstm_norm_g": mlstm_norm_g[0],
    }

    c_all = jnp.concatenate([c_prompt, c_sample], axis=0)
    ada_all = _ada(c_all, w_ada[0], b_ada[0], 1024).reshape(-1, N_ADA, d)
    fin_all = _ada(c_all, w_ada_final, b_ada_final, 1024).reshape(-1, 2, d)

    y_p, conv_p, C_p, n_p, m_p, _ = _trunk(
        x_prompt, ada_all[:Bp], fin_all[:Bp], None, None, W, bb=1, tt=256, rb=4, emit_v=False)
    m0 = jnp.broadcast_to(state_mlstm_m[0][:, :, None], state_mlstm_m.shape[1:] + (GATE_PAD,))
    y_s, conv_s, C_s, n_s, m_s, v_s = _trunk(
        x_sample, ada_all[Bp:], fin_all[Bp:], state_conv[0],
        (state_mlstm_C[0], state_mlstm_n[0], m0), W, bb=32, tt=8, rb=8, emit_v=True)
    return (y_p, y_s, C_p, n_p, m_p, conv_p, C_s, n_s, m_s, conv_s, v_s[None])
```

```python
import functools
import math

import jax
import jax.numpy as jnp
from jax import lax
from jax.experimental import pallas as pl
from jax.experimental.pallas import tpu as pltpu

D_MODEL = 1024
N_ADA = 9
GMLP_GROUPS = 4
GMLP_GROUP_DIM = D_MODEL // GMLP_GROUPS
GMLP_CHUNK = 128
MLSTM_HEADS = 4
MLSTM_HEAD_DIM = D_MODEL // MLSTM_HEADS
MLSTM_CHUNK = 128
CONV_W = 4
EPS = 1e-6
GATE_PAD = 128
HIST_ROWS = 8
VMEM_LIMIT_BYTES = 56 * 1024 * 1024

_BF16 = jnp.bfloat16
_F32 = jnp.float32


def _dot(a, b):
    return jnp.dot(a, b, preferred_element_type=_F32)


def _dot_nt(a, b):
    return lax.dot_general(a, b, (((1,), (1,)), ((), ())), preferred_element_type=_F32)


def _bdot(a, b, ca, cb):
    return lax.dot_general(a, b, (((ca,), (cb,)), ((0,), (0,))), preferred_element_type=_F32)


def _sigmoid(x):
    return 1.0 / (1.0 + jnp.exp(-x))


def _silu(x):
    return x * _sigmoid(x)


def _gelu_tanh(x):
    return 0.5 * x * (1.0 + jnp.tanh(0.7978845608028654 * (x + 0.044715 * (x * x * x))))


def _log_sigmoid(x):
    return jnp.minimum(x, 0.0) - jnp.log1p(jnp.exp(-jnp.abs(x)))


def _rms_mod(x, g, shift, scale):
    y = x * lax.rsqrt(jnp.mean(x * x, axis=-1, keepdims=True) + EPS) * g
    return y * (1.0 + scale) + shift


def _resident(shape):
    nd = len(shape)
    return pl.BlockSpec(shape, lambda *_: (0,) * nd, pipeline_mode=pl.Buffered(1))


def _params(semantics):
    return pltpu.CompilerParams(dimension_semantics=semantics, vmem_limit_bytes=VMEM_LIMIT_BYTES)


def _ada_kernel(c_ref, w_ref, b_ref, o_ref):
    cs = _silu(c_ref[...]).astype(_BF16)
    o_ref[...] = _dot(cs, w_ref[...].astype(_BF16)) + b_ref[...]


def _ada(c, w, b, bn):
    bc, d = c.shape
    n = w.shape[1]
    return pl.pallas_call(
        _ada_kernel,
        out_shape=jax.ShapeDtypeStruct((bc, n), _F32),
        grid=(n // bn,),
        in_specs=[pl.BlockSpec((bc, d), lambda j: (0, 0)),
                  pl.BlockSpec((d, bn), lambda j: (0, j)),
                  pl.BlockSpec((1, bn), lambda j: (0, j))],
        out_specs=pl.BlockSpec((bc, bn), lambda j: (0, j)),
        compiler_params=_params(("arbitrary",)),
        name="ada",
    )(c, w, b.reshape(1, n))


def _ffn_kernel(x_ref, ada_ref, g_ref, wa_ref, wb_ref, wo_ref, *rest, j, final):
    if final:
        fin_ref, gf_ref, o_ref = rest
    else:
        (o_ref,) = rest
    bb, tt, d = x_ref.shape
    x = x_ref[...]
    h = _rms_mod(x, g_ref[...], ada_ref[:, j:j + 1, :], ada_ref[:, j + 1:j + 2, :])
    hb = h.reshape(bb * tt, d).astype(_BF16)
    a = _dot(hb, wa_ref[...])
    b = _dot(hb, wb_ref[...])
    gated = (_silu(a) * b).astype(_BF16)
    out = _dot(gated, wo_ref[...]).reshape(bb, tt, d)
    y = x + 0.5 * ada_ref[:, j + 2:j + 3, :] * out
    if final:
        y = _rms_mod(y, gf_ref[...], fin_ref[:, 0:1, :], fin_ref[:, 1:2, :])
    o_ref[...] = y


def _ffn(x, ada, g, wa, wb, wo, *, j, bb, tt, fin=None, g_final=None):
    B, T, d = x.shape
    dff = wa.shape[1]
    final = fin is not None
    tok = lambda b, t: (b, t, 0)
    row = lambda b, t: (b, 0, 0)
    in_specs = [pl.BlockSpec((bb, tt, d), tok),
                pl.BlockSpec((bb, N_ADA, d), row),
                _resident((1, d)), _resident((d, dff)), _resident((d, dff)), _resident((dff, d))]
    args = [x, ada, g.reshape(1, d), wa, wb, wo]
    if final:
        in_specs += [pl.BlockSpec((bb, 2, d), row), _resident((1, d))]
        args += [fin, g_final.reshape(1, d)]
    return pl.pallas_call(
        functools.partial(_ffn_kernel, j=j, final=final),
        out_shape=jax.ShapeDtypeStruct((B, T, d), _F32),
        grid=(B // bb, T // tt),
        in_specs=in_specs,
        out_specs=pl.BlockSpec((bb, tt, d), tok),
        compiler_params=_params(("arbitrary", "arbitrary")),
        name="ffn_final" if final else "ffn",
    )(*args)


def _mix_in_kernel(x_ref, ada_ref, g_ref, wu_ref, wv_ref, wqk_ref, wvm_ref, wo_ref, wif_ref,
                   wga_ref, wgb_ref, bg_ref, cw_ref, cb_ref, lng_ref, lnb_ref, wt_ref, bias_ref,
                   *rest, L, zero_hist, emit_v):
    rest = list(rest)
    conv0_ref = None if zero_hist else rest.pop(0)
    ya_ref, gb_ref, q_ref, k_ref, vm_ref, gcol_ref, conv_ref = rest[:7]
    rest = rest[7:]
    v_ref = rest.pop(0) if emit_v else None
    (xp_ref,) = rest

    bb, tt, d = x_ref.shape
    M = bb * tt
    t_idx = pl.program_id(1)

    h = _rms_mod(x_ref[...], g_ref[...], ada_ref[:, 3:4, :], ada_ref[:, 4:5, :])
    hb = h.reshape(M, d).astype(_BF16)

    u = _gelu_tanh(_dot(hb, wu_ref[...]))
    gv = _gelu_tanh(_dot(hb, wv_ref[...]))
    mu = jnp.mean(gv, axis=-1, keepdims=True)
    var = jnp.mean(jnp.square(gv - mu), axis=-1, keepdims=True)
    v = (gv - mu) * lax.rsqrt(var + EPS) * lng_ref[...] + lnb_ref[...]
    if emit_v:
        v_ref[...] = v.reshape(bb, tt, d)
    vb = v.astype(_BF16)
    rows = lax.broadcasted_iota(jnp.int32, (M, M), 0)
    cols = lax.broadcasted_iota(jnp.int32, (M, M), 1)
    mask = jnp.logical_and(rows // L == cols // L, cols <= rows)
    sig_ga = _sigmoid(_dot(hb, wga_ref[...]))
    for g in range(GMLP_GROUPS):
        sl = slice(g * GMLP_GROUP_DIM, (g + 1) * GMLP_GROUP_DIM)
        wg = jnp.where(mask, wt_ref[g], 0.0).astype(_BF16)
        mixed = _dot(wg, vb[:, sl]) + bias_ref[:, g:g + 1]
        ya_ref[:, :, sl] = (sig_ga[:, sl] * u[:, sl] * mixed).reshape(bb, tt, GMLP_GROUP_DIM)

    zqk = _dot(hb, wqk_ref[...])
    nqk = zqk.shape[-1]

    @pl.when(t_idx == 0)
    def _():
        if zero_hist:
            xp_ref[:, 0:HIST_ROWS, :] = jnp.zeros((bb, HIST_ROWS, nqk), _F32)
        else:
            xp_ref[:, HIST_ROWS - (CONV_W - 1):HIST_ROWS, :] = conv0_ref[...]

    xp_ref[:, HIST_ROWS:HIST_ROWS + tt, :] = zqk.reshape(bb, tt, nqk)
    conv = cb_ref[...]
    for jj in range(CONV_W):
        lo = HIST_ROWS - (CONV_W - 1) + jj
        conv = conv + xp_ref[:, lo:lo + tt, :] * cw_ref[jj:jj + 1, :]
    hist = xp_ref[:, tt + HIST_ROWS - (CONV_W - 1):tt + HIST_ROWS, :]
    xp_ref[:, HIST_ROWS - (CONV_W - 1):HIST_ROWS, :] = hist
    conv_ref[...] = hist
    qk = _silu(conv)
    q_ref[...] = qk[:, :, :d]
    k_ref[...] = qk[:, :, d:]
    vm_ref[...] = _dot(hb, wvm_ref[...]).reshape(bb, tt, d)
    gate_b = _sigmoid(_dot(hb, wo_ref[...])) * _sigmoid(_dot(hb, wgb_ref[...]))
    gb_ref[...] = gate_b.reshape(bb, tt, d)
    gif = _dot(hb, wif_ref[...])[:, :2 * MLSTM_HEADS] + bg_ref[...]
    gcol_ref[...] = gif.reshape(bb, tt, 2 * MLSTM_HEADS)


def _mix_in(x, ada, g_mix, wts, b_gates, conv_w, conv_b, ln_g, ln_b, wtile, bias_big, conv0,
            *, bb, tt, L, emit_v):
    B, T, d = x.shape
    wu, wv, wqk, wvm, wo, wif, wga, wgb = wts
    nqk = wqk.shape[1]
    M = bb * tt
    zero_hist = conv0 is None
    tok = lambda b, t: (b, t, 0)
    row = lambda b, t: (b, 0, 0)
    in_specs = [pl.BlockSpec((bb, tt, d), tok), pl.BlockSpec((bb, N_ADA, d), row), _resident((1, d))]
    in_specs += [_resident(w.shape) for w in wts]
    in_specs += [_resident((1, 2 * MLSTM_HEADS)), _resident((CONV_W, nqk)), _resident((1, nqk)),
                 _resident((1, d)), _resident((1, d)), _resident((GMLP_GROUPS, M, M)),
                 _resident((M, GMLP_GROUPS))]
    args = [x, ada, g_mix.reshape(1, d), *wts, b_gates.reshape(1, -1), conv_w, conv_b.reshape(1, nqk),
            ln_g.reshape(1, d), ln_b.reshape(1, d), wtile, bias_big]
    if not zero_hist:
        in_specs.append(pl.BlockSpec((bb, CONV_W - 1, nqk), row))
        args.append(conv0)
    tok_out = jax.ShapeDtypeStruct((B, T, d), _F32)
    out_shape = [tok_out] * 5 + [jax.ShapeDtypeStruct((B, T, 2 * MLSTM_HEADS), _F32),
                                 jax.ShapeDtypeStruct((B, CONV_W - 1, nqk), _F32)]
    out_specs = [pl.BlockSpec((bb, tt, d), tok)] * 5 + [
        pl.BlockSpec((bb, tt, 2 * MLSTM_HEADS), tok), pl.BlockSpec((bb, CONV_W - 1, nqk), row)]
    if emit_v:
        out_shape.append(tok_out)
        out_specs.append(pl.BlockSpec((bb, tt, d), tok))
    return pl.pallas_call(
        functools.partial(_mix_in_kernel, L=L, zero_hist=zero_hist, emit_v=emit_v),
        out_shape=out_shape,
        grid=(B // bb, T // tt),
        in_specs=in_specs,
        out_specs=out_specs,
        scratch_shapes=[pltpu.VMEM((bb, tt + HIST_ROWS, nqk), _F32)],
        compiler_params=_params(("arbitrary", "arbitrary")),
        name="mix_in",
    )(*args)


def _mlstm_kernel(q_ref, k_ref, v_ref, gcol_ref, x_ref, ada_ref, ya_ref, gb_ref, ng_ref, wout_ref,
                  *rest, zero_state):
    if zero_state:
        o_ref, C_ref, n_ref, m_ref = rest
    else:
        C0_ref, n0_ref, m0_ref, o_ref, C_ref, n_ref, m_ref = rest
    rb, L, _ = q_ref.shape
    H, DH = MLSTM_HEADS, MLSTM_HEAD_DIM
    NB = rb * H

    @pl.when(pl.program_id(1) == 0)
    def _():
        if zero_state:
            C_ref[...] = jnp.zeros(C_ref.shape, _F32)
            n_ref[...] = jnp.zeros(n_ref.shape, _F32)
            m_ref[...] = jnp.zeros(m_ref.shape, _F32)
        else:
            C_ref[...] = C0_ref[...]
            n_ref[...] = n0_ref[...]
            m_ref[...] = m0_ref[...]

    ti = lax.broadcasted_iota(jnp.int32, (L, L), 0)
    si = lax.broadcasted_iota(jnp.int32, (L, L), 1)
    eye = ti == si
    causal = si <= ti

    chains = [(r, hd) for r in range(rb) for hd in range(H)]
    heads = lambda ref: jnp.stack([ref[r, :, hd * DH:(hd + 1) * DH] for r, hd in chains])
    qh = heads(q_ref)
    kh = heads(k_ref) * (DH ** -0.5)
    qb = qh.astype(_BF16)
    kb = kh.astype(_BF16)
    vb = heads(v_ref).astype(_BF16)
    Ch = C_ref[...].reshape(NB, DH, DH)
    nh = n_ref[...].reshape(NB, 1, DH)
    m_old = m_ref[...].reshape(NB, 1, GATE_PAD)[:, :, 0:1]
    gates = gcol_ref[...]
    i_col = jnp.stack([gates[r, :, hd:hd + 1] for r, hd in chains])
    lf_col = _log_sigmoid(jnp.stack([gates[r, :, H + hd:H + hd + 1] for r, hd in chains]))

    lf_row = jnp.sum(jnp.where(eye, lf_col, 0.0), axis=1, keepdims=True)
    b_col = jnp.sum(jnp.where(causal, lf_row, 0.0), axis=2, keepdims=True)
    c_col = i_col - b_col
    r_row = jnp.sum(jnp.where(eye, c_col, 0.0), axis=1, keepdims=True)
    b_last = b_col[:, L - 1:L, :]

    inter = b_col + m_old
    dm = jnp.where(causal, b_col + r_row, -jnp.inf)
    m_t = jnp.maximum(inter, jnp.max(dm, axis=2, keepdims=True))
    w_inter = jnp.exp(inter - m_t)
    s = _bdot(qb, kb, 2, 2) * jnp.exp(dm - m_t)
    num = w_inter * _bdot(qb, Ch.astype(_BF16), 2, 1) + _bdot(s.astype(_BF16), vb, 2, 1)
    den = (w_inter * jnp.sum(qh * nh, axis=2, keepdims=True)
           + jnp.sum(s, axis=2, keepdims=True))
    hh = num * (1.0 / jnp.maximum(jnp.abs(den), jnp.exp(-m_t)))

    m_new = jnp.maximum(b_last + m_old, jnp.max(b_last + r_row, axis=2, keepdims=True))
    a_prev = jnp.exp(b_last + m_old - m_new)
    kw = kh * jnp.exp(b_last + c_col - m_new)
    C_new = a_prev * Ch + _bdot(kw.astype(_BF16), vb, 1, 1)
    C_ref[...] = C_new.reshape(rb, H, DH, DH)
    n_ref[...] = (a_prev * nh + jnp.sum(kw, axis=1, keepdims=True)).reshape(rb, H, DH)
    m_ref[...] = jnp.broadcast_to(m_new, (NB, 1, GATE_PAD)).reshape(rb, H, GATE_PAD)

    mu = jnp.mean(hh, axis=2, keepdims=True)
    var = jnp.mean(jnp.square(hh - mu), axis=2, keepdims=True)
    ng = jnp.stack([ng_ref[:, hd * DH:(hd + 1) * DH] for _, hd in chains])
    hn = (hh - mu) * lax.rsqrt(var + EPS) * ng

    d = x_ref.shape[2]
    mix = jnp.concatenate(
        [jnp.concatenate([ya_ref[r, :, hd * DH:(hd + 1) * DH]
                          + gb_ref[r, :, hd * DH:(hd + 1) * DH] * hn[r * H + hd] for hd in range(H)], axis=1)
         for r in range(rb)], axis=0)
    out = _dot(mix.astype(_BF16), wout_ref[...]).reshape(rb, L, d)
    o_ref[...] = x_ref[...] + ada_ref[:, 5:6, :] * out


def _mlstm(q, k, v, gcol, x, ada, ya, gb, norm_g, w_out, state, *, L, rb):
    B, T, d = q.shape
    H, DH = MLSTM_HEADS, MLSTM_HEAD_DIM
    zero_state = state is None
    tok = lambda b, c: (b, c, 0)
    blk = pl.BlockSpec((rb, L, d), tok)
    in_specs = [blk] * 3 + [pl.BlockSpec((rb, L, 2 * H), tok), blk,
                            pl.BlockSpec((rb, N_ADA, d), lambda b, c: (b, 0, 0)), blk, blk,
                            _resident((1, d)), _resident((d, d))]
    args = [q, k, v, gcol, x, ada, ya, gb, norm_g.reshape(1, d), w_out]
    st_specs = [pl.BlockSpec((rb, H, DH, DH), lambda b, c: (b, 0, 0, 0)),
                pl.BlockSpec((rb, H, DH), lambda b, c: (b, 0, 0)),
                pl.BlockSpec((rb, H, GATE_PAD), lambda b, c: (b, 0, 0))]
    if not zero_state:
        in_specs += st_specs
        args += list(state)
    return pl.pallas_call(
        functools.partial(_mlstm_kernel, zero_state=zero_state),
        out_shape=[jax.ShapeDtypeStruct((B, T, d), _F32),
                   jax.ShapeDtypeStruct((B, H, DH, DH), _F32),
                   jax.ShapeDtypeStruct((B, H, DH), _F32),
                   jax.ShapeDtypeStruct((B, H, GATE_PAD), _F32)],
        grid=(B // rb, T // L),
        in_specs=in_specs,
        out_specs=[pl.BlockSpec((rb, L, d), tok)] + st_specs,
        compiler_params=_params(("arbitrary", "arbitrary")),
        name="mlstm",
    )(*args)


def _trunk(x, ada, fin, conv0, state, W, *, bb, tt, rb, emit_v):
    B, T, d = x.shape
    L = min(T, GMLP_CHUNK)
    assert L == math.gcd(T, MLSTM_CHUNK) and tt % L == 0 and tt % 8 == 0
    M = bb * tt
    wtile = jnp.tile(W["gmlp_ws"][:, :L, :L], (1, M // L, M // L))
    bias_big = jnp.tile(W["gmlp_bs"][:, :L].T, (M // L, 1))

    x = _ffn(x, ada, W["g_ffn1"], *W["ffn1"], j=0, bb=bb, tt=tt)
    outs = _mix_in(x, ada, W["g_mix"], W["w_in_parts"], W["b_gates"], W["conv_w"], W["conv_b"],
                   W["gmlp_ln_g"], W["gmlp_ln_b"], wtile, bias_big, conv0,
                   bb=bb, tt=tt, L=L, emit_v=emit_v)
    ya, gb, q, k, vm, gcol, conv_new = outs[:7]
    x, C, n, m = _mlstm(q, k, vm, gcol, x, ada, ya, gb, W["mlstm_norm_g"], W["w_out"], state,
                        L=L, rb=rb)
    y = _ffn(x, ada, W["g_ffn2"], *W["ffn2"], j=6, bb=bb, tt=tt, fin=fin, g_final=W["g_final"])
    v = outs[7] if emit_v else None
    return y, conv_new[None], C[None], n[None], m[None, :, :, 0], v


def kernel(x_prompt, x_sample, c_prompt, c_sample, state_mlstm_C, state_mlstm_n, state_mlstm_m, state_conv, w_ada, b_ada, g_ffn1, w_ffn1_in, w_ffn1_out, g_mix, w_in, b_gates, conv_w, conv_b, gmlp_ln_g, gmlp_ln_b, gmlp_ws, gmlp_bs, mlstm_norm_g, w_out, g_ffn2, w_ffn2_in, w_ffn2_out, w_ada_final, b_ada_final, g_final):
    assert w_ada.shape[0] == 1, "single-layer trunk only"
    d, H = D_MODEL, MLSTM_HEADS
    Bp = x_prompt.shape[0]
    bf = lambda a: a.astype(_BF16)

    def ffn_w(w_i, w_o):
        dff = w_o.shape[1]
        return bf(w_i[0, :, :dff]), bf(w_i[0, :, dff:]), bf(w_o[0])

    wi = w_in[0]
    s = [0, d, 2 * d, 4 * d, 5 * d, 6 * d, 6 * d + 2 * H, 7 * d + 2 * H, 8 * d + 2 * H]
    wu, wv, wqk, wvm, wo, wif, wga, wgb = [wi[:, s[i]:s[i + 1]] for i in range(8)]
    wif = jnp.pad(wif, ((0, 0), (0, GATE_PAD - 2 * H)))
    W = {
        "ffn1": ffn_w(w_ffn1_in, w_ffn1_out), "ffn2": ffn_w(w_ffn2_in, w_ffn2_out),
        "w_in_parts": tuple(bf(w) for w in (wu, wv, wqk, wvm, wo, wif, wga, wgb)),
        "w_out": bf(w_out[0]),
        "g_ffn1": g_ffn1[0], "g_mix": g_mix[0], "g_ffn2": g_ffn2[0], "g_final": g_final,
        "b_gates": b_gates[0], "conv_w": conv_w[0], "conv_b": conv_b[0],
        "gmlp_ln_g": gmlp_ln_g[0], "gmlp_ln_b": gmlp_ln_b[0],
        "gmlp_ws": gmlp_ws[0], "gmlp_bs": gmlp_bs[0], "mlstm_norm_g": mlstm_norm_g[0],
    }

    c_all = jnp.concatenate([c_prompt, c_sample], axis=0)
    ada_all = _ada(c_all, w_ada[0], b_ada[0], 1024).reshape(-1, N_ADA, d)
    fin_all = _ada(c_all, w_ada_final, b_ada_final, 1024).reshape(-1, 2, d)

    y_p, conv_p, C_p, n_p, m_p, _ = _trunk(
        x_prompt, ada_all[:Bp], fin_all[:Bp], None, None, W, bb=1, tt=256, rb=4, emit_v=False)
    m0 = jnp.broadcast_to(state_mlstm_m[0][:, :, None], state_mlstm_m.shape[1:] + (GATE_PAD,))
    y_s, conv_s, C_s, n_s, m_s, v_s = _trunk(
        x_sample, ada_all[Bp:], fin_all[Bp:], state_conv[0],
        (state_mlstm_C[0], state_mlstm_n[0], m0), W, bb=32, tt=8, rb=8, emit_v=True)
    return (y_p, y_s, C_p, n_p, m_p, conv_p, C_s, n_s, m_s, conv_s, v_s[None])
```

```python
import functools
import math

import jax
import jax.numpy as jnp
from jax import lax
from jax.experimental import pallas as pl
from jax.experimental.pallas import tpu as pltpu

D_MODEL = 1024
N_ADA = 9
GMLP_GROUPS = 4
GMLP_GROUP_DIM = D_MODEL // GMLP_GROUPS
GMLP_CHUNK = 128
MLSTM_HEADS = 4
MLSTM_HEAD_DIM = D_MODEL // MLSTM_HEADS
MLSTM_CHUNK = 128
CONV_W = 4
EPS = 1e-6
GATE_PAD = 128
HIST_ROWS = 8
VMEM_LIMIT_BYTES = 56 * 1024 * 1024

_BF16 = jnp.bfloat16
_F32 = jnp.float32


def _dot(a, b):
    return jnp.dot(a, b, preferred_element_type=_F32)


def _dot_nt(a, b):
    return lax.dot_general(a, b, (((1,), (1,)), ((), ())), preferred_element_type=_F32)


def _bdot(a, b, ca, cb):
    return lax.dot_general(a, b, (((ca,), (cb,)), ((0,), (0,))), preferred_element_type=_F32)


def _sigmoid(x):
    return 1.0 / (1.0 + jnp.exp(-x))


def _silu(x):
    return x * _sigmoid(x)


def _gelu_tanh(x):
    return 0.5 * x * (1.0 + jnp.tanh(0.7978845608028654 * (x + 0.044715 * (x * x * x))))


def _log_sigmoid(x):
    return jnp.minimum(x, 0.0) - jnp.log1p(jnp.exp(-jnp.abs(x)))


def _rms_mod(x, g, shift, scale):
    y = x * lax.rsqrt(jnp.mean(x * x, axis=-1, keepdims=True) + EPS) * g
    return y * (1.0 + scale) + shift


def _resident(shape):
    nd = len(shape)
    return pl.BlockSpec(shape, lambda *_: (0,) * nd, pipeline_mode=pl.Buffered(1))


def _params(semantics):
    return pltpu.CompilerParams(dimension_semantics=semantics, vmem_limit_bytes=VMEM_LIMIT_BYTES)


def _ada_kernel(c_ref, w_ref, b_ref, o_ref):
    cs = _silu(c_ref[...]).astype(_BF16)
    o_ref[...] = _dot(cs, w_ref[...].astype(_BF16)) + b_ref[...]


def _ada(c, w, b, bn):
    bc, d = c.shape
    n = w.shape[1]
    return pl.pallas_call(
        _ada_kernel,
        out_shape=jax.ShapeDtypeStruct((bc, n), _F32),
        grid=(n // bn,),
        in_specs=[pl.BlockSpec((bc, d), lambda j: (0, 0)),
                  pl.BlockSpec((d, bn), lambda j: (0, j)),
                  pl.BlockSpec((1, bn), lambda j: (0, j))],
        out_specs=pl.BlockSpec((bc, bn), lambda j: (0, j)),
        compiler_params=_params(("arbitrary",)),
        name="ada",
    )(c, w, b.reshape(1, n))


def _ffn_kernel(x_ref, ada_ref, g_ref, wi_ref, wo_ref, *rest, j, final):
    if final:
        fin_ref, gf_ref, o_ref = rest
    else:
        (o_ref,) = rest
    bb, tt, d = x_ref.shape
    dff = wo_ref.shape[0]
    x = x_ref[...]
    h = _rms_mod(x, g_ref[...], ada_ref[:, j:j + 1, :], ada_ref[:, j + 1:j + 2, :])
    hb = h.reshape(bb * tt, d).astype(_BF16)
    a = _dot(hb, wi_ref[:, :dff])
    b = _dot(hb, wi_ref[:, dff:])
    gated = (_silu(a) * b).astype(_BF16)
    out = _dot(gated, wo_ref[...]).reshape(bb, tt, d)
    y = x + 0.5 * ada_ref[:, j + 2:j + 3, :] * out
    if final:
        y = _rms_mod(y, gf_ref[...], fin_ref[:, 0:1, :], fin_ref[:, 1:2, :])
    o_ref[...] = y


def _ffn(x, ada, g, wi, wo, *, j, bb, tt, fin=None, g_final=None):
    B, T, d = x.shape
    dff = wo.shape[0]
    final = fin is not None
    tok = lambda b, t: (b, t, 0)
    row = lambda b, t: (b, 0, 0)
    in_specs = [pl.BlockSpec((bb, tt, d), tok),
                pl.BlockSpec((bb, N_ADA, d), row),
                _resident((1, d)), _resident((d, 2 * dff)), _resident((dff, d))]
    args = [x, ada, g.reshape(1, d), wi, wo]
    if final:
        in_specs += [pl.BlockSpec((bb, 2, d), row), _resident((1, d))]
        args += [fin, g_final.reshape(1, d)]
    return pl.pallas_call(
        functools.partial(_ffn_kernel, j=j, final=final),
        out_shape=jax.ShapeDtypeStruct((B, T, d), _F32),
        grid=(B // bb, T // tt),
        in_specs=in_specs,
        out_specs=pl.BlockSpec((bb, tt, d), tok),
        compiler_params=_params(("arbitrary", "arbitrary")),
        name="ffn_final" if final else "ffn",
    )(*args)


def _mix_in_kernel(x_ref, ada_ref, g_ref, w_ref, bg_ref, cw_ref, cb_ref, lng_ref, lnb_ref, wt_ref,
                   bias_ref, *rest, L, zero_hist, emit_v):
    dm = x_ref.shape[2]
    edges = [0, dm, 2 * dm, 4 * dm, 5 * dm, 6 * dm, 7 * dm, 8 * dm, 8 * dm + GATE_PAD]
    wu_ref, wv_ref, wqk_ref, wvm_ref, wo_ref, wga_ref, wgb_ref, wif_ref = [
        w_ref.at[:, lo:hi] for lo, hi in zip(edges[:-1], edges[1:])]
    rest = list(rest)
    conv0_ref = None if zero_hist else rest.pop(0)
    ya_ref, gb_ref, q_ref, k_ref, vm_ref, gcol_ref, conv_ref = rest[:7]
    rest = rest[7:]
    v_ref = rest.pop(0) if emit_v else None
    (xp_ref,) = rest

    bb, tt, d = x_ref.shape
    M = bb * tt
    t_idx = pl.program_id(1)

    h = _rms_mod(x_ref[...], g_ref[...], ada_ref[:, 3:4, :], ada_ref[:, 4:5, :])
    hb = h.reshape(M, d).astype(_BF16)

    u = _gelu_tanh(_dot(hb, wu_ref[...]))
    gv = _gelu_tanh(_dot(hb, wv_ref[...]))
    mu = jnp.mean(gv, axis=-1, keepdims=True)
    var = jnp.mean(jnp.square(gv - mu), axis=-1, keepdims=True)
    v = (gv - mu) * lax.rsqrt(var + EPS) * lng_ref[...] + lnb_ref[...]
    if emit_v:
        v_ref[...] = v.reshape(bb, tt, d)
    vb = v.astype(_BF16)
    rows = lax.broadcasted_iota(jnp.int32, (M, M), 0)
    cols = lax.broadcasted_iota(jnp.int32, (M, M), 1)
    mask = jnp.logical_and(rows // L == cols // L, cols <= rows)
    sig_ga = _sigmoid(_dot(hb, wga_ref[...]))
    for g in range(GMLP_GROUPS):
        sl = slice(g * GMLP_GROUP_DIM, (g + 1) * GMLP_GROUP_DIM)
        wg = jnp.where(mask, wt_ref[g], 0.0).astype(_BF16)
        mixed = _dot(wg, vb[:, sl]) + bias_ref[:, g:g + 1]
        ya_ref[:, :, sl] = (sig_ga[:, sl] * u[:, sl] * mixed).reshape(bb, tt, GMLP_GROUP_DIM)

    zqk = _dot(hb, wqk_ref[...])
    nqk = zqk.shape[-1]

    @pl.when(t_idx == 0)
    def _():
        if zero_hist:
            xp_ref[:, 0:HIST_ROWS, :] = jnp.zeros((bb, HIST_ROWS, nqk), _F32)
        else:
            xp_ref[:, HIST_ROWS - (CONV_W - 1):HIST_ROWS, :] = conv0_ref[...]

    xp_ref[:, HIST_ROWS:HIST_ROWS + tt, :] = zqk.reshape(bb, tt, nqk)
    conv = cb_ref[...]
    for jj in range(CONV_W):
        lo = HIST_ROWS - (CONV_W - 1) + jj
        conv = conv + xp_ref[:, lo:lo + tt, :] * cw_ref[jj:jj + 1, :]
    hist = xp_ref[:, tt + HIST_ROWS - (CONV_W - 1):tt + HIST_ROWS, :]
    xp_ref[:, HIST_ROWS - (CONV_W - 1):HIST_ROWS, :] = hist
    conv_ref[...] = hist
    qk = _silu(conv)
    q_ref[...] = qk[:, :, :d]
    k_ref[...] = qk[:, :, d:]
    vm_ref[...] = _dot(hb, wvm_ref[...]).reshape(bb, tt, d)
    gate_b = _sigmoid(_dot(hb, wo_ref[...])) * _sigmoid(_dot(hb, wgb_ref[...]))
    gb_ref[...] = gate_b.reshape(bb, tt, d)
    gif = _dot(hb, wif_ref[...])[:, :2 * MLSTM_HEADS] + bg_ref[...]
    gcol_ref[...] = gif.reshape(bb, tt, 2 * MLSTM_HEADS)


def _mix_in(x, ada, g_mix, w_in, b_gates, conv_w, conv_b, ln_g, ln_b, wtile, bias_big, conv0,
            *, bb, tt, L, emit_v):
    B, T, d = x.shape
    nqk = conv_w.shape[1]
    M = bb * tt
    zero_hist = conv0 is None
    tok = lambda b, t: (b, t, 0)
    row = lambda b, t: (b, 0, 0)
    in_specs = [pl.BlockSpec((bb, tt, d), tok), pl.BlockSpec((bb, N_ADA, d), row), _resident((1, d)),
                _resident(w_in.shape),
                _resident((1, 2 * MLSTM_HEADS)), _resident((CONV_W, nqk)), _resident((1, nqk)),
                _resident((1, d)), _resident((1, d)), _resident((GMLP_GROUPS, M, M)),
                _resident((M, GMLP_GROUPS))]
    args = [x, ada, g_mix.reshape(1, d), w_in, b_gates.reshape(1, -1), conv_w, conv_b.reshape(1, nqk),
            ln_g.reshape(1, d), ln_b.reshape(1, d), wtile, bias_big]
    if not zero_hist:
        in_specs.append(pl.BlockSpec((bb, CONV_W - 1, nqk), row))
        args.append(conv0)
    tok_out = jax.ShapeDtypeStruct((B, T, d), _F32)
    out_shape = [tok_out] * 5 + [jax.ShapeDtypeStruct((B, T, 2 * MLSTM_HEADS), _F32),
                                 jax.ShapeDtypeStruct((B, CONV_W - 1, nqk), _F32)]
    out_specs = [pl.BlockSpec((bb, tt, d), tok)] * 5 + [
        pl.BlockSpec((bb, tt, 2 * MLSTM_HEADS), tok), pl.BlockSpec((bb, CONV_W - 1, nqk), row)]
    if emit_v:
        out_shape.append(tok_out)
        out_specs.append(pl.BlockSpec((bb, tt, d), tok))
    return pl.pallas_call(
        functools.partial(_mix_in_kernel, L=L, zero_hist=zero_hist, emit_v=emit_v),
        out_shape=out_shape,
        grid=(B // bb, T // tt),
        in_specs=in_specs,
        out_specs=out_specs,
        scratch_shapes=[pltpu.VMEM((bb, tt + HIST_ROWS, nqk), _F32)],
        compiler_params=_params(("arbitrary", "arbitrary")),
        name="mix_in",
    )(*args)


def _mlstm_kernel(q_ref, k_ref, v_ref, gcol_ref, x_ref, ada_ref, ya_ref, gb_ref, ng_ref, wout_ref,
                  *rest, zero_state):
    if zero_state:
        o_ref, C_ref, n_ref, m_ref = rest
    else:
        C0_ref, n0_ref, m0_ref, o_ref, C_ref, n_ref, m_ref = rest
    rb, L, _ = q_ref.shape
    H, DH = MLSTM_HEADS, MLSTM_HEAD_DIM
    NB = rb * H

    @pl.when(pl.program_id(1) == 0)
    def _():
        if zero_state:
            C_ref[...] = jnp.zeros(C_ref.shape, _F32)
            n_ref[...] = jnp.zeros(n_ref.shape, _F32)
            m_ref[...] = jnp.zeros(m_ref.shape, _F32)
        else:
            C_ref[...] = C0_ref[...]
            n_ref[...] = n0_ref[...]
            m_ref[...] = m0_ref[...]

    ti = lax.broadcasted_iota(jnp.int32, (L, L), 0)
    si = lax.broadcasted_iota(jnp.int32, (L, L), 1)
    eye = ti == si
    causal = si <= ti

    chains = [(r, hd) for r in range(rb) for hd in range(H)]
    heads = lambda ref: jnp.stack([ref[r, :, hd * DH:(hd + 1) * DH] for r, hd in chains])
    qh = heads(q_ref)
    kh = heads(k_ref) * (DH ** -0.5)
    qb = qh.astype(_BF16)
    kb = kh.astype(_BF16)
    vb = heads(v_ref).astype(_BF16)
    Ch = C_ref[...].reshape(NB, DH, DH)
    nh = n_ref[...].reshape(NB, 1, DH)
    m_old = m_ref[...].reshape(NB, 1, GATE_PAD)[:, :, 0:1]
    gates = gcol_ref[...]
    i_col = jnp.stack([gates[r, :, hd:hd + 1] for r, hd in chains])
    lf_col = _log_sigmoid(jnp.stack([gates[r, :, H + hd:H + hd + 1] for r, hd in chains]))

    lf_row = jnp.sum(jnp.where(eye, lf_col, 0.0), axis=1, keepdims=True)
    b_col = jnp.sum(jnp.where(causal, lf_row, 0.0), axis=2, keepdims=True)
    c_col = i_col - b_col
    r_row = jnp.sum(jnp.where(eye, c_col, 0.0), axis=1, keepdims=True)
    b_last = b_col[:, L - 1:L, :]

    inter = b_col + m_old
    dm = jnp.where(causal, b_col + r_row, -jnp.inf)
    m_t = jnp.maximum(inter, jnp.max(dm, axis=2, keepdims=True))
    w_inter = jnp.exp(inter - m_t)
    s = _bdot(qb, kb, 2, 2) * jnp.exp(dm - m_t)
    num = w_inter * _bdot(qb, Ch.astype(_BF16), 2, 1) + _bdot(s.astype(_BF16), vb, 2, 1)
    den = (w_inter * jnp.sum(qh * nh, axis=2, keepdims=True)
           + jnp.sum(s, axis=2, keepdims=True))
    hh = num * (1.0 / jnp.maximum(jnp.abs(den), jnp.exp(-m_t)))

    m_new = jnp.maximum(b_last + m_old, jnp.max(b_last + r_row, axis=2, keepdims=True))
    a_prev = jnp.exp(b_last + m_old - m_new)
    kw = kh * jnp.exp(b_last + c_col - m_new)
    C_new = a_prev * Ch + _bdot(kw.astype(_BF16), vb, 1, 1)
    C_ref[...] = C_new.reshape(rb, H, DH, DH)
    n_ref[...] = (a_prev * nh + jnp.sum(kw, axis=1, keepdims=True)).reshape(rb, H, DH)
    m_ref[...] = jnp.broadcast_to(m_new, (NB, 1, GATE_PAD)).reshape(rb, H, GATE_PAD)

    mu = jnp.mean(hh, axis=2, keepdims=True)
    var = jnp.mean(jnp.square(hh - mu), axis=2, keepdims=True)
    ng = jnp.stack([ng_ref[:, hd * DH:(hd + 1) * DH] for _, hd in chains])
    hn = (hh - mu) * lax.rsqrt(var + EPS) * ng

    d = x_ref.shape[2]
    mix = jnp.concatenate(
        [jnp.concatenate([ya_ref[r, :, hd * DH:(hd + 1) * DH]
                          + gb_ref[r, :, hd * DH:(hd + 1) * DH] * hn[r * H + hd] for hd in range(H)], axis=1)
         for r in range(rb)], axis=0)
    out = _dot(mix.astype(_BF16), wout_ref[...]).reshape(rb, L, d)
    o_ref[...] = x_ref[...] + ada_ref[:, 5:6, :] * out


def _mlstm(q, k, v, gcol, x, ada, ya, gb, norm_g, w_out, state, *, L, rb):
    B, T, d = q.shape
    H, DH = MLSTM_HEADS, MLSTM_HEAD_DIM
    zero_state = state is None
    tok = lambda b, c: (b, c, 0)
    blk = pl.BlockSpec((rb, L, d), tok)
    in_specs = [blk] * 3 + [pl.BlockSpec((rb, L, 2 * H), tok), blk,
                            pl.BlockSpec((rb, N_ADA, d), lambda b, c: (b, 0, 0)), blk, blk,
                            _resident((1, d)), _resident((d, d))]
    args = [q, k, v, gcol, x, ada, ya, gb, norm_g.reshape(1, d), w_out]
    st_specs = [pl.BlockSpec((rb, H, DH, DH), lambda b, c: (b, 0, 0, 0)),
                pl.BlockSpec((rb, H, DH), lambda b, c: (b, 0, 0)),
                pl.BlockSpec((rb, H, GATE_PAD), lambda b, c: (b, 0, 0))]
    if not zero_state:
        in_specs += st_specs
        args += list(state)
    return pl.pallas_call(
        functools.partial(_mlstm_kernel, zero_state=zero_state),
        out_shape=[jax.ShapeDtypeStruct((B, T, d), _F32),
                   jax.ShapeDtypeStruct((B, H, DH, DH), _F32),
                   jax.ShapeDtypeStruct((B, H, DH), _F32),
                   jax.ShapeDtypeStruct((B, H, GATE_PAD), _F32)],
        grid=(B // rb, T // L),
        in_specs=in_specs,
        out_specs=[pl.BlockSpec((rb, L, d), tok)] + st_specs,
        compiler_params=_params(("arbitrary", "arbitrary")),
        name="mlstm",
    )(*args)


def _trunk(x, ada, fin, conv0, state, W, *, bb, tt, tt_ffn, rb, emit_v):
    B, T, d = x.shape
    L = min(T, GMLP_CHUNK)
    assert L == math.gcd(T, MLSTM_CHUNK) and tt % L == 0 and tt % 8 == 0
    M = bb * tt
    wrow = jnp.tile(W["gmlp_ws"][:, :L, :L], (1, 1, M // L))
    wtile = jnp.broadcast_to(wrow[:, None], (GMLP_GROUPS, M // L, L, M)).reshape(GMLP_GROUPS, M, M)
    bias_big = jnp.tile(W["gmlp_bs"][:, :L].T, (M // L, 1))

    x = _ffn(x, ada, W["g_ffn1"], *W["ffn1"], j=0, bb=bb, tt=tt_ffn)
    outs = _mix_in(x, ada, W["g_mix"], W["w_in"], W["b_gates"], W["conv_w"], W["conv_b"],
                   W["gmlp_ln_g"], W["gmlp_ln_b"], wtile, bias_big, conv0,
                   bb=bb, tt=tt, L=L, emit_v=emit_v)
    ya, gb, q, k, vm, gcol, conv_new = outs[:7]
    x, C, n, m = _mlstm(q, k, vm, gcol, x, ada, ya, gb, W["mlstm_norm_g"], W["w_out"], state,
                        L=L, rb=rb)
    y = _ffn(x, ada, W["g_ffn2"], *W["ffn2"], j=6, bb=bb, tt=tt_ffn, fin=fin, g_final=W["g_final"])
    v = outs[7] if emit_v else None
    return y, conv_new[None], C[None], n[None], m[None, :, :, 0], v


def kernel(x_prompt, x_sample, c_prompt, c_sample, state_mlstm_C, state_mlstm_n, state_mlstm_m, state_conv, w_ada, b_ada, g_ffn1, w_ffn1_in, w_ffn1_out, g_mix, w_in, b_gates, conv_w, conv_b, gmlp_ln_g, gmlp_ln_b, gmlp_ws, gmlp_bs, mlstm_norm_g, w_out, g_ffn2, w_ffn2_in, w_ffn2_out, w_ada_final, b_ada_final, g_final):
    assert w_ada.shape[0] == 1, "single-layer trunk only"
    d, H = D_MODEL, MLSTM_HEADS
    Bp = x_prompt.shape[0]
    bf = lambda a: a.astype(_BF16)

    wi = w_in[0]
    g0 = 6 * d
    w_in_re = jnp.concatenate(
        [wi[:, :g0], wi[:, g0 + 2 * H:], wi[:, g0:g0 + 2 * H],
         jnp.zeros((d, GATE_PAD - 2 * H), wi.dtype)], axis=1)
    W = {
        "ffn1": (bf(w_ffn1_in[0]), bf(w_ffn1_out[0])), "ffn2": (bf(w_ffn2_in[0]), bf(w_ffn2_out[0])),
        "w_in": bf(w_in_re),
        "w_out": bf(w_out[0]),
        "g_ffn1": g_ffn1[0], "g_mix": g_mix[0], "g_ffn2": g_ffn2[0], "g_final": g_final,
        "b_gates": b_gates[0], "conv_w": conv_w[0], "conv_b": conv_b[0],
        "gmlp_ln_g": gmlp_ln_g[0], "gmlp_ln_b": gmlp_ln_b[0],
        "gmlp_ws": gmlp_ws[0], "gmlp_bs": gmlp_bs[0], "mlstm_norm_g": mlstm_norm_g[0],
    }

    c_all = jnp.concatenate([c_prompt, c_sample], axis=0)
    ada_all = _ada(c_all, w_ada[0], b_ada[0], 1024).reshape(-1, N_ADA, d)
    fin_all = _ada(c_all, w_ada_final, b_ada_final, 1024).reshape(-1, 2, d)

    y_p, conv_p, C_p, n_p, m_p, _ = _trunk(
        x_prompt, ada_all[:Bp], fin_all[:Bp], None, None, W, bb=1, tt=256, tt_ffn=512, rb=4, emit_v=False)
    m0 = jnp.broadcast_to(state_mlstm_m[0][:, :, None], state_mlstm_m.shape[1:] + (GATE_PAD,))
    y_s, conv_s, C_s, n_s, m_s, v_s = _trunk(
        x_sample, ada_all[Bp:], fin_all[Bp:], state_conv[0],
        (state_mlstm_C[0], state_mlstm_n[0], m0), W, bb=32, tt=8, tt_ffn=8, rb=8, emit_v=True)
    return (y_p, y_s, C_p, n_p, m_p, conv_p, C_s, n_s, m_s, conv_s, v_s[None])
```

```python
import functools
import math

import jax
import jax.numpy as jnp
from jax import lax
from jax.experimental import pallas as pl
from jax.experimental.pallas import tpu as pltpu

D_MODEL = 1024
N_ADA = 9
GMLP_GROUPS = 4
GMLP_GROUP_DIM = D_MODEL // GMLP_GROUPS
GMLP_CHUNK = 128
MLSTM_HEADS = 4
MLSTM_HEAD_DIM = D_MODEL // MLSTM_HEADS
MLSTM_CHUNK = 128
CONV_W = 4
EPS = 1e-6
GATE_PAD = 128
HIST_ROWS = 8
VMEM_LIMIT_BYTES = 56 * 1024 * 1024

_BF16 = jnp.bfloat16
_F32 = jnp.float32


def _dot(a, b):
    return jnp.dot(a, b, preferred_element_type=_F32)


def _dot_nt(a, b):
    return lax.dot_general(a, b, (((1,), (1,)), ((), ())), preferred_element_type=_F32)


def _bdot(a, b, ca, cb):
    return lax.dot_general(a, b, (((ca,), (cb,)), ((0,), (0,))), preferred_element_type=_F32)


def _sigmoid(x):
    return 1.0 / (1.0 + jnp.exp(-x))


def _silu(x):
    return x * _sigmoid(x)


def _gelu_tanh(x):
    return 0.5 * x * (1.0 + jnp.tanh(0.7978845608028654 * (x + 0.044715 * (x * x * x))))


def _log_sigmoid(x):
    return jnp.minimum(x, 0.0) - jnp.log1p(jnp.exp(-jnp.abs(x)))


def _rms_mod(x, g, shift, scale):
    y = x * lax.rsqrt(jnp.mean(x * x, axis=-1, keepdims=True) + EPS) * g
    return y * (1.0 + scale) + shift


def _resident(shape):
    nd = len(shape)
    return pl.BlockSpec(shape, lambda *_: (0,) * nd, pipeline_mode=pl.Buffered(1))


def _params(semantics):
    return pltpu.CompilerParams(dimension_semantics=semantics, vmem_limit_bytes=VMEM_LIMIT_BYTES)


def _ada_kernel(c_ref, w_ref, b_ref, o_ref):
    cs = _silu(c_ref[...]).astype(_BF16)
    o_ref[...] = _dot(cs, w_ref[...].astype(_BF16)) + b_ref[...]


def _ada(c, w, b, bn):
    bc, d = c.shape
    n = w.shape[1]
    return pl.pallas_call(
        _ada_kernel,
        out_shape=jax.ShapeDtypeStruct((bc, n), _F32),
        grid=(n // bn,),
        in_specs=[pl.BlockSpec((bc, d), lambda j: (0, 0)),
                  pl.BlockSpec((d, bn), lambda j: (0, j)),
                  pl.BlockSpec((1, bn), lambda j: (0, j))],
        out_specs=pl.BlockSpec((bc, bn), lambda j: (0, j)),
        compiler_params=_params(("arbitrary",)),
        name="ada",
    )(c, w, b.reshape(1, n))


def _ffn_kernel(x_ref, ada_ref, g_ref, wi_ref, wo_ref, *rest, j, final):
    if final:
        fin_ref, gf_ref, o_ref = rest
    else:
        (o_ref,) = rest
    bb, tt, d = x_ref.shape
    dff = wo_ref.shape[0]
    x = x_ref[...]
    h = _rms_mod(x, g_ref[...], ada_ref[:, j:j + 1, :], ada_ref[:, j + 1:j + 2, :])
    hb = h.reshape(bb * tt, d).astype(_BF16)
    a = _dot(hb, wi_ref[:, :dff])
    b = _dot(hb, wi_ref[:, dff:])
    gated = (_silu(a) * b).astype(_BF16)
    out = _dot(gated, wo_ref[...]).reshape(bb, tt, d)
    y = x + 0.5 * ada_ref[:, j + 2:j + 3, :] * out
    if final:
        y = _rms_mod(y, gf_ref[...], fin_ref[:, 0:1, :], fin_ref[:, 1:2, :])
    o_ref[...] = y


def _ffn(x, ada, g, wi, wo, *, j, bb, tt, fin=None, g_final=None):
    B, T, d = x.shape
    dff = wo.shape[0]
    final = fin is not None
    tok = lambda b, t: (b, t, 0)
    row = lambda b, t: (b, 0, 0)
    in_specs = [pl.BlockSpec((bb, tt, d), tok),
                pl.BlockSpec((bb, N_ADA, d), row),
                _resident((1, d)), _resident((d, 2 * dff)), _resident((dff, d))]
    args = [x, ada, g.reshape(1, d), wi, wo]
    if final:
        in_specs += [pl.BlockSpec((bb, 2, d), row), _resident((1, d))]
        args += [fin, g_final.reshape(1, d)]
    return pl.pallas_call(
        functools.partial(_ffn_kernel, j=j, final=final),
        out_shape=jax.ShapeDtypeStruct((B, T, d), _F32),
        grid=(B // bb, T // tt),
        in_specs=in_specs,
        out_specs=pl.BlockSpec((bb, tt, d), tok),
        compiler_params=_params(("arbitrary", "arbitrary")),
        name="ffn_final" if final else "ffn",
    )(*args)


def _mix_in_kernel(x_ref, ada_ref, g_ref, w_ref, bg_ref, cw_ref, cb_ref, lng_ref, lnb_ref, wt_ref,
                   bias_ref, *rest, L, zero_hist, emit_v):
    dm = x_ref.shape[2]
    edges = [0, dm, 2 * dm, 4 * dm, 5 * dm, 6 * dm, 7 * dm, 8 * dm, 8 * dm + GATE_PAD]
    wu_ref, wv_ref, wqk_ref, wvm_ref, wo_ref, wga_ref, wgb_ref, wif_ref = [
        w_ref.at[:, lo:hi] for lo, hi in zip(edges[:-1], edges[1:])]
    rest = list(rest)
    conv0_ref = None if zero_hist else rest.pop(0)
    ya_ref, gb_ref, q_ref, k_ref, vm_ref, gcol_ref, conv_ref = rest[:7]
    rest = rest[7:]
    v_ref = rest.pop(0) if emit_v else None
    (hist_ref,) = rest

    bb, tt, d = x_ref.shape
    M = bb * tt
    t_idx = pl.program_id(1)

    h = _rms_mod(x_ref[...], g_ref[...], ada_ref[:, 3:4, :], ada_ref[:, 4:5, :])
    hb = h.reshape(M, d).astype(_BF16)

    gv = _gelu_tanh(_dot(hb, wv_ref[...]))
    mu = jnp.mean(gv, axis=-1, keepdims=True)
    var = jnp.mean(jnp.square(gv - mu), axis=-1, keepdims=True)
    v = (gv - mu) * lax.rsqrt(var + EPS) * lng_ref[...] + lnb_ref[...]
    if emit_v:
        v_ref[...] = v.reshape(bb, tt, d)
    vb = v.astype(_BF16)
    rows = lax.broadcasted_iota(jnp.int32, (M, M), 0)
    cols = lax.broadcasted_iota(jnp.int32, (M, M), 1)
    mask = jnp.logical_and(rows // L == cols // L, cols <= rows)
    CW = GMLP_GROUP_DIM
    for g in range(GMLP_GROUPS):
        sl = slice(g * CW, (g + 1) * CW)
        wg = jnp.where(mask, wt_ref[g], 0.0).astype(_BF16)
        mixed = _dot(wg, vb[:, sl]) + bias_ref[:, g:g + 1]
        u_g = _gelu_tanh(_dot(hb, wu_ref[:, sl]))
        s_g = _sigmoid(_dot(hb, wga_ref[:, sl]))
        ya_ref[:, :, sl] = (s_g * u_g * mixed).reshape(bb, tt, CW)

    nqk = cw_ref.shape[1]

    @pl.when(t_idx == 0)
    def _():
        hist_ref[...] = jnp.zeros((bb, HIST_ROWS, nqk), _F32)
        if not zero_hist:
            hist_ref[:, HIST_ROWS - (CONV_W - 1):HIST_ROWS, :] = conv0_ref[...]

    t8 = lax.broadcasted_iota(jnp.int32, (bb, HIST_ROWS, CW), 1)
    for c in range(nqk // CW):
        cs = slice(c * CW, (c + 1) * CW)
        z3 = _dot(hb, wqk_ref[:, cs]).reshape(bb, tt, CW)
        hist = hist_ref[:, :, cs]
        conv = cb_ref[:, cs] + z3 * cw_ref[CONV_W - 1:CONV_W, cs]
        for jj in range(1, CONV_W):
            rolled = pltpu.roll(z3, jj, axis=1)
            first = jnp.where(t8 < jj, pltpu.roll(hist, jj, axis=1), rolled[:, :HIST_ROWS, :])
            shifted = first if tt == HIST_ROWS else jnp.concatenate([first, rolled[:, HIST_ROWS:, :]], axis=1)
            conv = conv + shifted * cw_ref[CONV_W - 1 - jj:CONV_W - jj, cs]
        hist_ref[:, :, cs] = z3[:, tt - HIST_ROWS:, :]
        conv_ref[:, :, cs] = z3[:, tt - (CONV_W - 1):, :]
        o_ref, lo = (q_ref, c * CW) if c * CW < d else (k_ref, c * CW - d)
        o_ref[:, :, lo:lo + CW] = _silu(conv)
    for c in range(d // CW):
        cs = slice(c * CW, (c + 1) * CW)
        vm_ref[:, :, cs] = _dot(hb, wvm_ref[:, cs]).reshape(bb, tt, CW)
        gate_b = _sigmoid(_dot(hb, wo_ref[:, cs])) * _sigmoid(_dot(hb, wgb_ref[:, cs]))
        gb_ref[:, :, cs] = gate_b.reshape(bb, tt, CW)
    gif = _dot(hb, wif_ref[...])[:, :2 * MLSTM_HEADS] + bg_ref[...]
    gcol_ref[...] = gif.reshape(bb, tt, 2 * MLSTM_HEADS)


def _mix_in(x, ada, g_mix, w_in, b_gates, conv_w, conv_b, ln_g, ln_b, wtile, bias_big, conv0,
            *, bb, tt, L, emit_v):
    B, T, d = x.shape
    nqk = conv_w.shape[1]
    M = bb * tt
    zero_hist = conv0 is None
    tok = lambda b, t: (b, t, 0)
    row = lambda b, t: (b, 0, 0)
    in_specs = [pl.BlockSpec((bb, tt, d), tok), pl.BlockSpec((bb, N_ADA, d), row), _resident((1, d)),
                _resident(w_in.shape),
                _resident((1, 2 * MLSTM_HEADS)), _resident((CONV_W, nqk)), _resident((1, nqk)),
                _resident((1, d)), _resident((1, d)), _resident((GMLP_GROUPS, M, M)),
                _resident((M, GMLP_GROUPS))]
    args = [x, ada, g_mix.reshape(1, d), w_in, b_gates.reshape(1, -1), conv_w, conv_b.reshape(1, nqk),
            ln_g.reshape(1, d), ln_b.reshape(1, d), wtile, bias_big]
    if not zero_hist:
        in_specs.append(pl.BlockSpec((bb, CONV_W - 1, nqk), row))
        args.append(conv0)
    tok_out = jax.ShapeDtypeStruct((B, T, d), _F32)
    out_shape = [tok_out] * 5 + [jax.ShapeDtypeStruct((B, T, 2 * MLSTM_HEADS), _F32),
                                 jax.ShapeDtypeStruct((B, CONV_W - 1, nqk), _F32)]
    out_specs = [pl.BlockSpec((bb, tt, d), tok)] * 5 + [
        pl.BlockSpec((bb, tt, 2 * MLSTM_HEADS), tok), pl.BlockSpec((bb, CONV_W - 1, nqk), row)]
    if emit_v:
        out_shape.append(tok_out)
        out_specs.append(pl.BlockSpec((bb, tt, d), tok))
    return pl.pallas_call(
        functools.partial(_mix_in_kernel, L=L, zero_hist=zero_hist, emit_v=emit_v),
        out_shape=out_shape,
        grid=(B // bb, T // tt),
        in_specs=in_specs,
        out_specs=out_specs,
        scratch_shapes=[pltpu.VMEM((bb, HIST_ROWS, nqk), _F32)],
        compiler_params=_params(("arbitrary", "arbitrary")),
        name="mix_in",
    )(*args)


def _mlstm_kernel(q_ref, k_ref, v_ref, gcol_ref, x_ref, ada_ref, ya_ref, gb_ref, ng_ref, wout_ref,
                  *rest, zero_state):
    if zero_state:
        o_ref, C_ref, n_ref, m_ref = rest
    else:
        C0_ref, n0_ref, m0_ref, o_ref, C_ref, n_ref, m_ref = rest
    rb, L, _ = q_ref.shape
    H, DH = MLSTM_HEADS, MLSTM_HEAD_DIM
    NB = rb * H

    @pl.when(pl.program_id(1) == 0)
    def _():
        if zero_state:
            C_ref[...] = jnp.zeros(C_ref.shape, _F32)
            n_ref[...] = jnp.zeros(n_ref.shape, _F32)
            m_ref[...] = jnp.zeros(m_ref.shape, _F32)
        else:
            C_ref[...] = C0_ref[...]
            n_ref[...] = n0_ref[...]
            m_ref[...] = m0_ref[...]

    ti = lax.broadcasted_iota(jnp.int32, (L, L), 0)
    si = lax.broadcasted_iota(jnp.int32, (L, L), 1)
    eye = ti == si
    causal = si <= ti

    chains = [(r, hd) for r in range(rb) for hd in range(H)]
    heads = lambda ref: jnp.stack([ref[r, :, hd * DH:(hd + 1) * DH] for r, hd in chains])
    qh = heads(q_ref)
    kh = heads(k_ref) * (DH ** -0.5)
    qb = qh.astype(_BF16)
    kb = kh.astype(_BF16)
    vb = heads(v_ref).astype(_BF16)
    Ch = C_ref[...].reshape(NB, DH, DH)
    nh = n_ref[...].reshape(NB, 1, DH)
    m_old = m_ref[...].reshape(NB, 1, GATE_PAD)[:, :, 0:1]
    gates = gcol_ref[...]
    i_col = jnp.stack([gates[r, :, hd:hd + 1] for r, hd in chains])
    lf_col = _log_sigmoid(jnp.stack([gates[r, :, H + hd:H + hd + 1] for r, hd in chains]))

    lf_row = jnp.sum(jnp.where(eye, lf_col, 0.0), axis=1, keepdims=True)
    b_col = jnp.sum(jnp.where(causal, lf_row, 0.0), axis=2, keepdims=True)
    c_col = i_col - b_col
    r_row = jnp.sum(jnp.where(eye, c_col, 0.0), axis=1, keepdims=True)
    b_last = b_col[:, L - 1:L, :]

    inter = b_col + m_old
    dm = jnp.where(causal, b_col + r_row, -jnp.inf)
    m_t = jnp.maximum(inter, jnp.max(dm, axis=2, keepdims=True))
    w_inter = jnp.exp(inter - m_t)
    s = _bdot(qb, kb, 2, 2) * jnp.exp(dm - m_t)
    num = w_inter * _bdot(qb, Ch.astype(_BF16), 2, 1) + _bdot(s.astype(_BF16), vb, 2, 1)
    den = (w_inter * jnp.sum(qh * nh, axis=2, keepdims=True)
           + jnp.sum(s, axis=2, keepdims=True))
    hh = num * (1.0 / jnp.maximum(jnp.abs(den), jnp.exp(-m_t)))

    m_new = jnp.maximum(b_last + m_old, jnp.max(b_last + r_row, axis=2, keepdims=True))
    a_prev = jnp.exp(b_last + m_old - m_new)
    kw = kh * jnp.exp(b_last + c_col - m_new)
    C_new = a_prev * Ch + _bdot(kw.astype(_BF16), vb, 1, 1)
    C_ref[...] = C_new.reshape(rb, H, DH, DH)
    n_ref[...] = (a_prev * nh + jnp.sum(kw, axis=1, keepdims=True)).reshape(rb, H, DH)
    m_ref[...] = jnp.broadcast_to(m_new, (NB, 1, GATE_PAD)).reshape(rb, H, GATE_PAD)

    mu = jnp.mean(hh, axis=2, keepdims=True)
    var = jnp.mean(jnp.square(hh - mu), axis=2, keepdims=True)
    ng = jnp.stack([ng_ref[:, hd * DH:(hd + 1) * DH] for _, hd in chains])
    hn = (hh - mu) * lax.rsqrt(var + EPS) * ng

    d = x_ref.shape[2]
    mix = jnp.concatenate(
        [jnp.concatenate([ya_ref[r, :, hd * DH:(hd + 1) * DH]
                          + gb_ref[r, :, hd * DH:(hd + 1) * DH] * hn[r * H + hd] for hd in range(H)], axis=1)
         for r in range(rb)], axis=0)
    out = _dot(mix.astype(_BF16), wout_ref[...]).reshape(rb, L, d)
    o_ref[...] = x_ref[...] + ada_ref[:, 5:6, :] * out


def _mlstm(q, k, v, gcol, x, ada, ya, gb, norm_g, w_out, state, *, L, rb):
    B, T, d = q.shape
    H, DH = MLSTM_HEADS, MLSTM_HEAD_DIM
    zero_state = state is None
    tok = lambda b, c: (b, c, 0)
    blk = pl.BlockSpec((rb, L, d), tok)
    in_specs = [blk] * 3 + [pl.BlockSpec((rb, L, 2 * H), tok), blk,
                            pl.BlockSpec((rb, N_ADA, d), lambda b, c: (b, 0, 0)), blk, blk,
                            _resident((1, d)), _resident((d, d))]
    args = [q, k, v, gcol, x, ada, ya, gb, norm_g.reshape(1, d), w_out]
    st_specs = [pl.BlockSpec((rb, H, DH, DH), lambda b, c: (b, 0, 0, 0)),
                pl.BlockSpec((rb, H, DH), lambda b, c: (b, 0, 0)),
                pl.BlockSpec((rb, H, GATE_PAD), lambda b, c: (b, 0, 0))]
    if not zero_state:
        in_specs += st_specs
        args += list(state)
    return pl.pallas_call(
        functools.partial(_mlstm_kernel, zero_state=zero_state),
        out_shape=[jax.ShapeDtypeStruct((B, T, d), _F32),
                   jax.ShapeDtypeStruct((B, H, DH, DH), _F32),
                   jax.ShapeDtypeStruct((B, H, DH), _F32),
                   jax.ShapeDtypeStruct((B, H, GATE_PAD), _F32)],
        grid=(B // rb, T // L),
        in_specs=in_specs,
        out_specs=[pl.BlockSpec((rb, L, d), tok)] + st_specs,
        compiler_params=_params(("arbitrary", "arbitrary")),
        name="mlstm",
    )(*args)


def _trunk(x, ada, fin, conv0, state, W, *, bb, tt, tt_ffn, rb, emit_v):
    B, T, d = x.shape
    L = min(T, GMLP_CHUNK)
    assert L == math.gcd(T, MLSTM_CHUNK) and tt % L == 0 and tt % 8 == 0
    M = bb * tt
    wrow = jnp.tile(W["gmlp_ws"][:, :L, :L], (1, 1, M // L))
    wtile = jnp.broadcast_to(wrow[:, None], (GMLP_GROUPS, M // L, L, M)).reshape(GMLP_GROUPS, M, M)
    bias_big = jnp.tile(W["gmlp_bs"][:, :L].T, (M // L, 1))

    x = _ffn(x, ada, W["g_ffn1"], *W["ffn1"], j=0, bb=bb, tt=tt_ffn)
    outs = _mix_in(x, ada, W["g_mix"], W["w_in"], W["b_gates"], W["conv_w"], W["conv_b"],
                   W["gmlp_ln_g"], W["gmlp_ln_b"], wtile, bias_big, conv0,
                   bb=bb, tt=tt, L=L, emit_v=emit_v)
    ya, gb, q, k, vm, gcol, conv_new = outs[:7]
    x, C, n, m = _mlstm(q, k, vm, gcol, x, ada, ya, gb, W["mlstm_norm_g"], W["w_out"], state,
                        L=L, rb=rb)
    y = _ffn(x, ada, W["g_ffn2"], *W["ffn2"], j=6, bb=bb, tt=tt_ffn, fin=fin, g_final=W["g_final"])
    v = outs[7] if emit_v else None
    return y, conv_new[None], C[None], n[None], m[None, :, :, 0], v


def kernel(x_prompt, x_sample, c_prompt, c_sample, state_mlstm_C, state_mlstm_n, state_mlstm_m, state_conv, w_ada, b_ada, g_ffn1, w_ffn1_in, w_ffn1_out, g_mix, w_in, b_gates, conv_w, conv_b, gmlp_ln_g, gmlp_ln_b, gmlp_ws, gmlp_bs, mlstm_norm_g, w_out, g_ffn2, w_ffn2_in, w_ffn2_out, w_ada_final, b_ada_final, g_final):
    assert w_ada.shape[0] == 1, "single-layer trunk only"
    d, H = D_MODEL, MLSTM_HEADS
    Bp = x_prompt.shape[0]
    bf = lambda a: a.astype(_BF16)

    wi = w_in[0]
    g0 = 6 * d
    w_in_re = jnp.concatenate(
        [wi[:, :g0], wi[:, g0 + 2 * H:], wi[:, g0:g0 + 2 * H],
         jnp.zeros((d, GATE_PAD - 2 * H), wi.dtype)], axis=1)
    W = {
        "ffn1": (bf(w_ffn1_in[0]), bf(w_ffn1_out[0])), "ffn2": (bf(w_ffn2_in[0]), bf(w_ffn2_out[0])),
        "w_in": bf(w_in_re),
        "w_out": bf(w_out[0]),
        "g_ffn1": g_ffn1[0], "g_mix": g_mix[0], "g_ffn2": g_ffn2[0], "g_final": g_final,
        "b_gates": b_gates[0], "conv_w": conv_w[0], "conv_b": conv_b[0],
        "gmlp_ln_g": gmlp_ln_g[0], "gmlp_ln_b": gmlp_ln_b[0],
        "gmlp_ws": gmlp_ws[0], "gmlp_bs": gmlp_bs[0], "mlstm_norm_g": mlstm_norm_g[0],
    }

    c_all = jnp.concatenate([c_prompt, c_sample], axis=0)
    ada_all = _ada(c_all, w_ada[0], b_ada[0], 1024).reshape(-1, N_ADA, d)
    fin_all = _ada(c_all, w_ada_final, b_ada_final, 1024).reshape(-1, 2, d)

    y_p, conv_p, C_p, n_p, m_p, _ = _trunk(
        x_prompt, ada_all[:Bp], fin_all[:Bp], None, None, W, bb=1, tt=256, tt_ffn=512, rb=4, emit_v=False)
    m0 = jnp.broadcast_to(state_mlstm_m[0][:, :, None], state_mlstm_m.shape[1:] + (GATE_PAD,))
    y_s, conv_s, C_s, n_s, m_s, v_s = _trunk(
        x_sample, ada_all[Bp:], fin_all[Bp:], state_conv[0],
        (state_mlstm_C[0], state_mlstm_n[0], m0), W, bb=32, tt=8, tt_ffn=8, rb=8, emit_v=True)
    return (y_p, y_s, C_p, n_p, m_p, conv_p, C_s, n_s, m_s, conv_s, v_s[None])
```

```python
import functools
import math

import jax
import jax.numpy as jnp
from jax import lax
from jax.experimental import pallas as pl
from jax.experimental.pallas import tpu as pltpu

D_MODEL = 1024
N_ADA = 9
GMLP_GROUPS = 4
GMLP_GROUP_DIM = D_MODEL // GMLP_GROUPS
GMLP_CHUNK = 128
MLSTM_HEADS = 4
MLSTM_HEAD_DIM = D_MODEL // MLSTM_HEADS
MLSTM_CHUNK = 128
CONV_W = 4
EPS = 1e-6
GATE_PAD = 128
HIST_ROWS = 8
VMEM_LIMIT_BYTES = 56 * 1024 * 1024

_BF16 = jnp.bfloat16
_F32 = jnp.float32


def _dot(a, b):
    return jnp.dot(a, b, preferred_element_type=_F32)


def _dot_nt(a, b):
    return lax.dot_general(a, b, (((1,), (1,)), ((), ())), preferred_element_type=_F32)


def _bdot(a, b, ca, cb):
    return lax.dot_general(a, b, (((ca,), (cb,)), ((0,), (0,))), preferred_element_type=_F32)


def _sigmoid(x):
    return 1.0 / (1.0 + jnp.exp(-x))


def _silu(x):
    return x * _sigmoid(x)


def _gelu_tanh(x):
    return 0.5 * x * (1.0 + jnp.tanh(0.7978845608028654 * (x + 0.044715 * (x * x * x))))


def _log_sigmoid(x):
    return jnp.minimum(x, 0.0) - jnp.log1p(jnp.exp(-jnp.abs(x)))


def _rms_mod(x, g, shift, scale):
    y = x * lax.rsqrt(jnp.mean(x * x, axis=-1, keepdims=True) + EPS) * g
    return y * (1.0 + scale) + shift


def _resident(shape):
    nd = len(shape)
    return pl.BlockSpec(shape, lambda *_: (0,) * nd, pipeline_mode=pl.Buffered(1))


def _params(semantics):
    return pltpu.CompilerParams(dimension_semantics=semantics, vmem_limit_bytes=VMEM_LIMIT_BYTES)


def _ada_kernel(c_ref, w_ref, b_ref, o_ref):
    cs = _silu(c_ref[...]).astype(_BF16)
    o_ref[...] = _dot(cs, w_ref[...].astype(_BF16)) + b_ref[...]


def _ada(c, w, b, bn):
    bc, d = c.shape
    n = w.shape[1]
    return pl.pallas_call(
        _ada_kernel,
        out_shape=jax.ShapeDtypeStruct((bc, n), _F32),
        grid=(n // bn,),
        in_specs=[pl.BlockSpec((bc, d), lambda j: (0, 0)),
                  pl.BlockSpec((d, bn), lambda j: (0, j)),
                  pl.BlockSpec((1, bn), lambda j: (0, j))],
        out_specs=pl.BlockSpec((bc, bn), lambda j: (0, j)),
        compiler_params=_params(("arbitrary",)),
        name="ada",
    )(c, w, b.reshape(1, n))


def _ffn_kernel(x_ref, ada_ref, g_ref, wi_ref, wo_ref, *rest, j, final):
    if final:
        fin_ref, gf_ref, o_ref = rest
    else:
        (o_ref,) = rest
    bb, tt, d = x_ref.shape
    dff = wo_ref.shape[0]
    x = x_ref[...]
    h = _rms_mod(x, g_ref[...], ada_ref[:, j:j + 1, :], ada_ref[:, j + 1:j + 2, :])
    hb = h.reshape(bb * tt, d).astype(_BF16)
    a = _dot(hb, wi_ref[:, :dff])
    b = _dot(hb, wi_ref[:, dff:])
    gated = (_silu(a) * b).astype(_BF16)
    out = _dot(gated, wo_ref[...]).reshape(bb, tt, d)
    y = x + 0.5 * ada_ref[:, j + 2:j + 3, :] * out
    if final:
        y = _rms_mod(y, gf_ref[...], fin_ref[:, 0:1, :], fin_ref[:, 1:2, :])
    o_ref[...] = y


def _ada_rows(ada_row0, bb):
    assert ada_row0 % bb == 0
    return lambda b, t: (b + ada_row0 // bb, 0, 0)


def _ffn(x, ada, g, wi, wo, *, j, bb, tt, ada_row0, fin=None, g_final=None):
    B, T, d = x.shape
    dff = wo.shape[0]
    final = fin is not None
    tok = lambda b, t: (b, t, 0)
    row = _ada_rows(ada_row0, bb)
    in_specs = [pl.BlockSpec((bb, tt, d), tok),
                pl.BlockSpec((bb, N_ADA, d), row),
                _resident((1, d)), _resident((d, 2 * dff)), _resident((dff, d))]
    args = [x, ada, g.reshape(1, d), wi, wo]
    if final:
        in_specs += [pl.BlockSpec((bb, 2, d), row), _resident((1, d))]
        args += [fin, g_final.reshape(1, d)]
    return pl.pallas_call(
        functools.partial(_ffn_kernel, j=j, final=final),
        out_shape=jax.ShapeDtypeStruct((B, T, d), _F32),
        grid=(B // bb, T // tt),
        in_specs=in_specs,
        out_specs=pl.BlockSpec((bb, tt, d), tok),
        compiler_params=_params(("arbitrary", "arbitrary")),
        name="ffn_final" if final else "ffn",
    )(*args)


def _mix_in_kernel(x_ref, ada_ref, g_ref, w_ref, bg_ref, cw_ref, cb_ref, lng_ref, lnb_ref, wt_ref,
                   bias_ref, *rest, L, zero_hist, emit_v):
    dm = x_ref.shape[2]
    edges = [0, dm, 2 * dm, 4 * dm, 5 * dm, 6 * dm, 7 * dm, 8 * dm, 8 * dm + GATE_PAD]
    wu_ref, wv_ref, wqk_ref, wvm_ref, wo_ref, wga_ref, wgb_ref, wif_ref = [
        w_ref.at[:, lo:hi] for lo, hi in zip(edges[:-1], edges[1:])]
    rest = list(rest)
    conv0_ref = None if zero_hist else rest.pop(0)
    ya_ref, gb_ref, q_ref, k_ref, vm_ref, gcol_ref, conv_ref = rest[:7]
    rest = rest[7:]
    v_ref = rest.pop(0) if emit_v else None
    (hist_ref,) = rest

    bb, tt, d = x_ref.shape
    M = bb * tt
    t_idx = pl.program_id(1)

    h = _rms_mod(x_ref[...], g_ref[...], ada_ref[:, 3:4, :], ada_ref[:, 4:5, :])
    hb = h.reshape(M, d).astype(_BF16)

    gv = _gelu_tanh(_dot(hb, wv_ref[...]))
    mu = jnp.mean(gv, axis=-1, keepdims=True)
    var = jnp.mean(jnp.square(gv - mu), axis=-1, keepdims=True)
    v = (gv - mu) * lax.rsqrt(var + EPS) * lng_ref[...] + lnb_ref[...]
    if emit_v:
        v_ref[...] = v.reshape(bb, tt, d)
    vb = v.astype(_BF16)
    rows = lax.broadcasted_iota(jnp.int32, (M, M), 0)
    cols = lax.broadcasted_iota(jnp.int32, (M, M), 1)
    mask = jnp.logical_and(rows // L == cols // L, cols <= rows)
    CW = GMLP_GROUP_DIM
    for g in range(GMLP_GROUPS):
        sl = slice(g * CW, (g + 1) * CW)
        wg = jnp.where(mask, wt_ref[g], 0.0).astype(_BF16)
        mixed = _dot(wg, vb[:, sl]) + bias_ref[:, g:g + 1]
        u_g = _gelu_tanh(_dot(hb, wu_ref[:, sl]))
        s_g = _sigmoid(_dot(hb, wga_ref[:, sl]))
        ya_ref[:, :, sl] = (s_g * u_g * mixed).reshape(bb, tt, CW)

    nqk = cw_ref.shape[1]

    @pl.when(t_idx == 0)
    def _():
        hist_ref[...] = jnp.zeros((bb, HIST_ROWS, nqk), _F32)
        if not zero_hist:
            hist_ref[:, HIST_ROWS - (CONV_W - 1):HIST_ROWS, :] = conv0_ref[...]

    t8 = lax.broadcasted_iota(jnp.int32, (bb, HIST_ROWS, CW), 1)
    for c in range(nqk // CW):
        cs = slice(c * CW, (c + 1) * CW)
        z3 = _dot(hb, wqk_ref[:, cs]).reshape(bb, tt, CW)
        hist = hist_ref[:, :, cs]
        conv = cb_ref[:, cs] + z3 * cw_ref[CONV_W - 1:CONV_W, cs]
        for jj in range(1, CONV_W):
            rolled = pltpu.roll(z3, jj, axis=1)
            first = jnp.where(t8 < jj, pltpu.roll(hist, jj, axis=1), rolled[:, :HIST_ROWS, :])
            shifted = first if tt == HIST_ROWS else jnp.concatenate([first, rolled[:, HIST_ROWS:, :]], axis=1)
            conv = conv + shifted * cw_ref[CONV_W - 1 - jj:CONV_W - jj, cs]
        hist_ref[:, :, cs] = z3[:, tt - HIST_ROWS:, :]
        conv_ref[:, :, cs] = z3[:, tt - (CONV_W - 1):, :]
        o_ref, lo = (q_ref, c * CW) if c * CW < d else (k_ref, c * CW - d)
        o_ref[:, :, lo:lo + CW] = _silu(conv)
    for c in range(d // CW):
        cs = slice(c * CW, (c + 1) * CW)
        vm_ref[:, :, cs] = _dot(hb, wvm_ref[:, cs]).reshape(bb, tt, CW)
        gate_b = _sigmoid(_dot(hb, wo_ref[:, cs])) * _sigmoid(_dot(hb, wgb_ref[:, cs]))
        gb_ref[:, :, cs] = gate_b.reshape(bb, tt, CW)
    gif = _dot(hb, wif_ref[...])[:, :2 * MLSTM_HEADS] + bg_ref[...]
    gcol_ref[...] = gif.reshape(bb, tt, 2 * MLSTM_HEADS)


def _mix_in(x, ada, g_mix, w_in, b_gates, conv_w, conv_b, ln_g, ln_b, wtile, bias_big, conv0,
            *, bb, tt, L, ada_row0, emit_v):
    B, T, d = x.shape
    nqk = conv_w.shape[1]
    M = bb * tt
    zero_hist = conv0 is None
    tok = lambda b, t: (b, t, 0)
    row = lambda b, t: (b, 0, 0)
    in_specs = [pl.BlockSpec((bb, tt, d), tok), pl.BlockSpec((bb, N_ADA, d), _ada_rows(ada_row0, bb)),
                _resident((1, d)), _resident(w_in.shape),
                _resident((1, 2 * MLSTM_HEADS)), _resident((CONV_W, nqk)), _resident((1, nqk)),
                _resident((1, d)), _resident((1, d)), _resident((GMLP_GROUPS, M, M)),
                _resident((M, GMLP_GROUPS))]
    args = [x, ada, g_mix.reshape(1, d), w_in, b_gates.reshape(1, -1), conv_w, conv_b.reshape(1, nqk),
            ln_g.reshape(1, d), ln_b.reshape(1, d), wtile, bias_big]
    if not zero_hist:
        in_specs.append(pl.BlockSpec((bb, CONV_W - 1, nqk), row))
        args.append(conv0)
    tok_out = jax.ShapeDtypeStruct((B, T, d), _F32)
    out_shape = [tok_out] * 5 + [jax.ShapeDtypeStruct((B, T, 2 * MLSTM_HEADS), _F32),
                                 jax.ShapeDtypeStruct((B, CONV_W - 1, nqk), _F32)]
    out_specs = [pl.BlockSpec((bb, tt, d), tok)] * 5 + [
        pl.BlockSpec((bb, tt, 2 * MLSTM_HEADS), tok), pl.BlockSpec((bb, CONV_W - 1, nqk), row)]
    if emit_v:
        out_shape.append(tok_out)
        out_specs.append(pl.BlockSpec((bb, tt, d), tok))
    return pl.pallas_call(
        functools.partial(_mix_in_kernel, L=L, zero_hist=zero_hist, emit_v=emit_v),
        out_shape=out_shape,
        grid=(B // bb, T // tt),
        in_specs=in_specs,
        out_specs=out_specs,
        scratch_shapes=[pltpu.VMEM((bb, HIST_ROWS, nqk), _F32)],
        compiler_params=_params(("arbitrary", "arbitrary")),
        name="mix_in",
    )(*args)


def _mlstm_kernel(q_ref, k_ref, v_ref, gcol_ref, x_ref, ada_ref, ya_ref, gb_ref, ng_ref, wout_ref,
                  *rest, zero_state):
    if zero_state:
        o_ref, C_ref, n_ref, m_ref = rest
    else:
        C0_ref, n0_ref, m0_ref, o_ref, C_ref, n_ref, m_ref = rest
    rb, L, _ = q_ref.shape
    H, DH = MLSTM_HEADS, MLSTM_HEAD_DIM
    NB = rb * H

    @pl.when(pl.program_id(1) == 0)
    def _():
        if zero_state:
            C_ref[...] = jnp.zeros(C_ref.shape, _F32)
            n_ref[...] = jnp.zeros(n_ref.shape, _F32)
            m_ref[...] = jnp.zeros(m_ref.shape, _F32)
        else:
            C_ref[...] = C0_ref[...]
            n_ref[...] = n0_ref[...]
            m_ref[...] = m0_ref[...]

    ti = lax.broadcasted_iota(jnp.int32, (L, L), 0)
    si = lax.broadcasted_iota(jnp.int32, (L, L), 1)
    eye = ti == si
    causal = si <= ti

    chains = [(r, hd) for r in range(rb) for hd in range(H)]
    heads = lambda ref: jnp.stack([ref[r, :, hd * DH:(hd + 1) * DH] for r, hd in chains])
    qh = heads(q_ref)
    kh = heads(k_ref) * (DH ** -0.5)
    qb = qh.astype(_BF16)
    kb = kh.astype(_BF16)
    vb = heads(v_ref).astype(_BF16)
    Ch = C_ref[...].reshape(NB, DH, DH)
    nh = n_ref[...].reshape(NB, 1, DH)
    m_old = m_ref[...].reshape(NB, 1, GATE_PAD)[:, :, 0:1]
    gates = gcol_ref[...]
    i_col = jnp.stack([gates[r, :, hd:hd + 1] for r, hd in chains])
    lf_col = _log_sigmoid(jnp.stack([gates[r, :, H + hd:H + hd + 1] for r, hd in chains]))

    lf_row = jnp.sum(jnp.where(eye, lf_col, 0.0), axis=1, keepdims=True)
    b_col = jnp.sum(jnp.where(causal, lf_row, 0.0), axis=2, keepdims=True)
    c_col = i_col - b_col
    r_row = jnp.sum(jnp.where(eye, c_col, 0.0), axis=1, keepdims=True)
    b_last = b_col[:, L - 1:L, :]

    inter = b_col + m_old
    dm = jnp.where(causal, b_col + r_row, -jnp.inf)
    m_t = jnp.maximum(inter, jnp.max(dm, axis=2, keepdims=True))
    w_inter = jnp.exp(inter - m_t)
    s = _bdot(qb, kb, 2, 2) * jnp.exp(dm - m_t)
    num = w_inter * _bdot(qb, Ch.astype(_BF16), 2, 1) + _bdot(s.astype(_BF16), vb, 2, 1)
    den = (w_inter * jnp.sum(qh * nh, axis=2, keepdims=True)
           + jnp.sum(s, axis=2, keepdims=True))
    hh = num * (1.0 / jnp.maximum(jnp.abs(den), jnp.exp(-m_t)))

    m_new = jnp.maximum(b_last + m_old, jnp.max(b_last + r_row, axis=2, keepdims=True))
    a_prev = jnp.exp(b_last + m_old - m_new)
    kw = kh * jnp.exp(b_last + c_col - m_new)
    C_new = a_prev * Ch + _bdot(kw.astype(_BF16), vb, 1, 1)
    C_ref[...] = C_new.reshape(rb, H, DH, DH)
    n_ref[...] = (a_prev * nh + jnp.sum(kw, axis=1, keepdims=True)).reshape(rb, H, DH)
    m_ref[...] = jnp.broadcast_to(m_new, (NB, 1, GATE_PAD)).reshape(rb, H, GATE_PAD)

    mu = jnp.mean(hh, axis=2, keepdims=True)
    var = jnp.mean(jnp.square(hh - mu), axis=2, keepdims=True)
    ng = jnp.stack([ng_ref[:, hd * DH:(hd + 1) * DH] for _, hd in chains])
    hn = (hh - mu) * lax.rsqrt(var + EPS) * ng

    d = x_ref.shape[2]
    mix = jnp.concatenate(
        [jnp.concatenate([ya_ref[r, :, hd * DH:(hd + 1) * DH]
                          + gb_ref[r, :, hd * DH:(hd + 1) * DH] * hn[r * H + hd] for hd in range(H)], axis=1)
         for r in range(rb)], axis=0)
    out = _dot(mix.astype(_BF16), wout_ref[...]).reshape(rb, L, d)
    o_ref[...] = x_ref[...] + ada_ref[:, 5:6, :] * out


def _mlstm(q, k, v, gcol, x, ada, ya, gb, norm_g, w_out, state, *, L, rb, ada_row0):
    B, T, d = q.shape
    H, DH = MLSTM_HEADS, MLSTM_HEAD_DIM
    zero_state = state is None
    tok = lambda b, c: (b, c, 0)
    blk = pl.BlockSpec((rb, L, d), tok)
    in_specs = [blk] * 3 + [pl.BlockSpec((rb, L, 2 * H), tok), blk,
                            pl.BlockSpec((rb, N_ADA, d), _ada_rows(ada_row0, rb)), blk, blk,
                            _resident((1, d)), _resident((d, d))]
    args = [q, k, v, gcol, x, ada, ya, gb, norm_g.reshape(1, d), w_out]
    st_specs = [pl.BlockSpec((rb, H, DH, DH), lambda b, c: (b, 0, 0, 0)),
                pl.BlockSpec((rb, H, DH), lambda b, c: (b, 0, 0)),
                pl.BlockSpec((rb, H, GATE_PAD), lambda b, c: (b, 0, 0))]
    if not zero_state:
        in_specs += st_specs
        args += list(state)
    return pl.pallas_call(
        functools.partial(_mlstm_kernel, zero_state=zero_state),
        out_shape=[jax.ShapeDtypeStruct((B, T, d), _F32),
                   jax.ShapeDtypeStruct((B, H, DH, DH), _F32),
                   jax.ShapeDtypeStruct((B, H, DH), _F32),
                   jax.ShapeDtypeStruct((B, H, GATE_PAD), _F32)],
        grid=(B // rb, T // L),
        in_specs=in_specs,
        out_specs=[pl.BlockSpec((rb, L, d), tok)] + st_specs,
        compiler_params=_params(("arbitrary", "arbitrary")),
        name="mlstm",
    )(*args)


def _trunk(x, ada, fin, conv0, state, W, *, ada_row0, bb, tt, tt_ffn, rb, emit_v):
    B, T, d = x.shape
    L = min(T, GMLP_CHUNK)
    assert L == math.gcd(T, MLSTM_CHUNK) and tt % L == 0 and tt % 8 == 0
    M = bb * tt
    wrow = jnp.tile(W["gmlp_ws"][:, :L, :L], (1, 1, M // L))
    wtile = jnp.broadcast_to(wrow[:, None], (GMLP_GROUPS, M // L, L, M)).reshape(GMLP_GROUPS, M, M)
    bias_big = jnp.tile(W["gmlp_bs"][:, :L].T, (M // L, 1))

    x = _ffn(x, ada, W["g_ffn1"], *W["ffn1"], j=0, bb=bb, tt=tt_ffn, ada_row0=ada_row0)
    outs = _mix_in(x, ada, W["g_mix"], W["w_in"], W["b_gates"], W["conv_w"], W["conv_b"],
                   W["gmlp_ln_g"], W["gmlp_ln_b"], wtile, bias_big, conv0,
                   bb=bb, tt=tt, L=L, ada_row0=ada_row0, emit_v=emit_v)
    ya, gb, q, k, vm, gcol, conv_new = outs[:7]
    x, C, n, m = _mlstm(q, k, vm, gcol, x, ada, ya, gb, W["mlstm_norm_g"], W["w_out"], state,
                        L=L, rb=rb, ada_row0=ada_row0)
    y = _ffn(x, ada, W["g_ffn2"], *W["ffn2"], j=6, bb=bb, tt=tt_ffn, ada_row0=ada_row0,
             fin=fin, g_final=W["g_final"])
    v = outs[7] if emit_v else None
    return y, conv_new[None], C[None], n[None], m[None, :, :, 0], v


def kernel(x_prompt, x_sample, c_prompt, c_sample, state_mlstm_C, state_mlstm_n, state_mlstm_m, state_conv, w_ada, b_ada, g_ffn1, w_ffn1_in, w_ffn1_out, g_mix, w_in, b_gates, conv_w, conv_b, gmlp_ln_g, gmlp_ln_b, gmlp_ws, gmlp_bs, mlstm_norm_g, w_out, g_ffn2, w_ffn2_in, w_ffn2_out, w_ada_final, b_ada_final, g_final):
    assert w_ada.shape[0] == 1, "single-layer trunk only"
    d, H = D_MODEL, MLSTM_HEADS
    bf = lambda a: a.astype(_BF16)

    wi = w_in[0]
    g0 = 6 * d
    w_in_re = jnp.concatenate(
        [wi[:, :g0], wi[:, g0 + 2 * H:], wi[:, g0:g0 + 2 * H],
         jnp.zeros((d, GATE_PAD - 2 * H), wi.dtype)], axis=1)
    W = {
        "ffn1": (bf(w_ffn1_in[0]), bf(w_ffn1_out[0])), "ffn2": (bf(w_ffn2_in[0]), bf(w_ffn2_out[0])),
        "w_in": bf(w_in_re),
        "w_out": bf(w_out[0]),
        "g_ffn1": g_ffn1[0], "g_mix": g_mix[0], "g_ffn2": g_ffn2[0], "g_final": g_final,
        "b_gates": b_gates[0], "conv_w": conv_w[0], "conv_b": conv_b[0],
        "gmlp_ln_g": gmlp_ln_g[0], "gmlp_ln_b": gmlp_ln_b[0],
        "gmlp_ws": gmlp_ws[0], "gmlp_bs": gmlp_bs[0], "mlstm_norm_g": mlstm_norm_g[0],
    }

    Bs = x_sample.shape[0]
    c_all = jnp.concatenate([c_sample, c_prompt], axis=0)
    ada_all = _ada(c_all, w_ada[0], b_ada[0], 1024).reshape(-1, N_ADA, d)
    fin_all = _ada(c_all, w_ada_final, b_ada_final, 1024).reshape(-1, 2, d)

    y_p, conv_p, C_p, n_p, m_p, _ = _trunk(
        x_prompt, ada_all, fin_all, None, None, W,
        ada_row0=Bs, bb=1, tt=512, tt_ffn=512, rb=4, emit_v=False)
    m0 = jnp.broadcast_to(state_mlstm_m[0][:, :, None], state_mlstm_m.shape[1:] + (GATE_PAD,))
    y_s, conv_s, C_s, n_s, m_s, v_s = _trunk(
        x_sample, ada_all, fin_all, state_conv[0],
        (state_mlstm_C[0], state_mlstm_n[0], m0), W,
        ada_row0=0, bb=32, tt=8, tt_ffn=8, rb=8, emit_v=True)
    return (y_p, y_s, C_p, n_p, m_p, conv_p, C_s, n_s, m_s, conv_s, v_s[None])
```

```python
import functools
import math

import jax
import jax.numpy as jnp
from jax import lax
from jax.experimental import pallas as pl
from jax.experimental.pallas import tpu as pltpu

D_MODEL = 1024
N_ADA = 9
GMLP_GROUPS = 4
GMLP_GROUP_DIM = D_MODEL // GMLP_GROUPS
GMLP_CHUNK = 128
MLSTM_HEADS = 4
MLSTM_HEAD_DIM = D_MODEL // MLSTM_HEADS
MLSTM_CHUNK = 128
CONV_W = 4
EPS = 1e-6
GATE_PAD = 128
HIST_ROWS = 8
VMEM_LIMIT_BYTES = 56 * 1024 * 1024

_BF16 = jnp.bfloat16
_F32 = jnp.float32


def _dot(a, b):
    return jnp.dot(a, b, preferred_element_type=_F32)


def _dot_nt(a, b):
    return lax.dot_general(a, b, (((1,), (1,)), ((), ())), preferred_element_type=_F32)


def _bdot(a, b, ca, cb):
    return lax.dot_general(a, b, (((ca,), (cb,)), ((0,), (0,))), preferred_element_type=_F32)


def _sigmoid(x):
    return 1.0 / (1.0 + jnp.exp(-x))


def _silu(x):
    return x * _sigmoid(x)


def _gelu_tanh(x):
    return 0.5 * x * (1.0 + jnp.tanh(0.7978845608028654 * (x + 0.044715 * (x * x * x))))


def _log_sigmoid(x):
    return jnp.minimum(x, 0.0) - jnp.log1p(jnp.exp(-jnp.abs(x)))


def _rms_mod(x, g, shift, scale):
    y = x * lax.rsqrt(jnp.mean(x * x, axis=-1, keepdims=True) + EPS) * g
    return y * (1.0 + scale) + shift


def _resident(shape):
    nd = len(shape)
    return pl.BlockSpec(shape, lambda *_: (0,) * nd, pipeline_mode=pl.Buffered(1))


def _params(semantics):
    return pltpu.CompilerParams(dimension_semantics=semantics, vmem_limit_bytes=VMEM_LIMIT_BYTES)


def _ada_kernel(c_ref, w_ref, b_ref, o_ref):
    cs = _silu(c_ref[...]).astype(_BF16)
    o_ref[...] = _dot(cs, w_ref[...].astype(_BF16)) + b_ref[...]


def _ada(c, w, b, bn):
    bc, d = c.shape
    n = w.shape[1]
    return pl.pallas_call(
        _ada_kernel,
        out_shape=jax.ShapeDtypeStruct((bc, n), _F32),
        grid=(n // bn,),
        in_specs=[pl.BlockSpec((bc, d), lambda j: (0, 0)),
                  pl.BlockSpec((d, bn), lambda j: (0, j)),
                  pl.BlockSpec((1, bn), lambda j: (0, j))],
        out_specs=pl.BlockSpec((bc, bn), lambda j: (0, j)),
        compiler_params=_params(("arbitrary",)),
        name="ada",
    )(c, w, b.reshape(1, n))


def _ffn_kernel(x_ref, ada_ref, g_ref, wi_ref, wo_ref, *rest, j, final):
    if final:
        fin_ref, gf_ref, o_ref = rest
    else:
        (o_ref,) = rest
    bb, tt, d = x_ref.shape
    dff = wo_ref.shape[0]
    x = x_ref[...]
    h = _rms_mod(x, g_ref[...], ada_ref[:, j:j + 1, :], ada_ref[:, j + 1:j + 2, :])
    hb = h.reshape(bb * tt, d).astype(_BF16)
    a = _dot(hb, wi_ref[:, :dff])
    b = _dot(hb, wi_ref[:, dff:])
    gated = (_silu(a) * b).astype(_BF16)
    out = _dot(gated, wo_ref[...]).reshape(bb, tt, d)
    y = x + 0.5 * ada_ref[:, j + 2:j + 3, :] * out
    if final:
        y = _rms_mod(y, gf_ref[...], fin_ref[:, 0:1, :], fin_ref[:, 1:2, :])
    o_ref[...] = y


def _ada_rows(ada_row0, bb):
    assert ada_row0 % bb == 0
    return lambda b, t: (b + ada_row0 // bb, 0, 0)


def _ffn(x, ada, g, wi, wo, *, j, bb, tt, ada_row0, fin=None, g_final=None):
    B, T, d = x.shape
    dff = wo.shape[0]
    final = fin is not None
    tok = lambda b, t: (b, t, 0)
    row = _ada_rows(ada_row0, bb)
    in_specs = [pl.BlockSpec((bb, tt, d), tok),
                pl.BlockSpec((bb, N_ADA, d), row),
                _resident((1, d)), _resident((d, 2 * dff)), _resident((dff, d))]
    args = [x, ada, g.reshape(1, d), wi, wo]
    if final:
        in_specs += [pl.BlockSpec((bb, 2, d), row), _resident((1, d))]
        args += [fin, g_final.reshape(1, d)]
    return pl.pallas_call(
        functools.partial(_ffn_kernel, j=j, final=final),
        out_shape=jax.ShapeDtypeStruct((B, T, d), _F32),
        grid=(B // bb, T // tt),
        in_specs=in_specs,
        out_specs=pl.BlockSpec((bb, tt, d), tok),
        compiler_params=_params(("arbitrary", "arbitrary")),
        name="ffn_final" if final else "ffn",
    )(*args)


def _prep_w_in_kernel(w_ref, o_ref):
    rows, n_in = w_ref.shape
    g0 = 6 * D_MODEL
    ng = 2 * MLSTM_HEADS
    w = w_ref[...]
    o_ref[:, :g0] = w[:, :g0].astype(_BF16)
    o_ref[:, g0:n_in - ng] = w[:, g0 + ng:].astype(_BF16)
    tail = jnp.concatenate([w[:, g0:g0 + ng], jnp.zeros((rows, GATE_PAD - ng), _F32)], axis=1)
    o_ref[:, n_in - ng:] = tail.astype(_BF16)


def _prep_w_in(w, rows):
    d, n_in = w.shape
    n_out = n_in - 2 * MLSTM_HEADS + GATE_PAD
    return pl.pallas_call(
        _prep_w_in_kernel,
        out_shape=jax.ShapeDtypeStruct((d, n_out), _BF16),
        grid=(d // rows,),
        in_specs=[pl.BlockSpec((rows, n_in), lambda i: (i, 0))],
        out_specs=pl.BlockSpec((rows, n_out), lambda i: (i, 0)),
        compiler_params=_params(("arbitrary",)),
        name="prep_w_in",
    )(w)


def _mix_in_kernel(x_ref, ada_ref, g_ref, w_ref, bg_ref, cw_ref, cb_ref, lng_ref, lnb_ref, wt_ref,
                   bias_ref, *rest, L, zero_hist, emit_v):
    dm = x_ref.shape[2]
    edges = [0, dm, 2 * dm, 4 * dm, 5 * dm, 6 * dm, 7 * dm, 8 * dm, 8 * dm + GATE_PAD]
    wu_ref, wv_ref, wqk_ref, wvm_ref, wo_ref, wga_ref, wgb_ref, wif_ref = [
        w_ref.at[:, lo:hi] for lo, hi in zip(edges[:-1], edges[1:])]
    rest = list(rest)
    conv0_ref = None if zero_hist else rest.pop(0)
    ya_ref, gb_ref, q_ref, k_ref, vm_ref, gcol_ref, conv_ref = rest[:7]
    rest = rest[7:]
    v_ref = rest.pop(0) if emit_v else None
    (hist_ref,) = rest

    bb, tt, d = x_ref.shape
    M = bb * tt
    t_idx = pl.program_id(1)

    h = _rms_mod(x_ref[...], g_ref[...], ada_ref[:, 3:4, :], ada_ref[:, 4:5, :])
    hb = h.reshape(M, d).astype(_BF16)

    gv = _gelu_tanh(_dot(hb, wv_ref[...]))
    mu = jnp.mean(gv, axis=-1, keepdims=True)
    var = jnp.mean(jnp.square(gv - mu), axis=-1, keepdims=True)
    v = (gv - mu) * lax.rsqrt(var + EPS) * lng_ref[...] + lnb_ref[...]
    if emit_v:
        v_ref[...] = v.reshape(bb, tt, d)
    vb = v.astype(_BF16)
    rows = lax.broadcasted_iota(jnp.int32, (M, M), 0)
    cols = lax.broadcasted_iota(jnp.int32, (M, M), 1)
    mask = jnp.logical_and(rows // L == cols // L, cols <= rows)
    CW = GMLP_GROUP_DIM
    for g in range(GMLP_GROUPS):
        sl = slice(g * CW, (g + 1) * CW)
        wg = jnp.where(mask, wt_ref[g], 0.0).astype(_BF16)
        mixed = _dot(wg, vb[:, sl]) + bias_ref[:, g:g + 1]
        u_g = _gelu_tanh(_dot(hb, wu_ref[:, sl]))
        s_g = _sigmoid(_dot(hb, wga_ref[:, sl]))
        ya_ref[:, :, sl] = (s_g * u_g * mixed).reshape(bb, tt, CW)

    nqk = cw_ref.shape[1]

    @pl.when(t_idx == 0)
    def _():
        hist_ref[...] = jnp.zeros((bb, HIST_ROWS, nqk), _F32)
        if not zero_hist:
            hist_ref[:, HIST_ROWS - (CONV_W - 1):HIST_ROWS, :] = conv0_ref[...]

    t8 = lax.broadcasted_iota(jnp.int32, (bb, HIST_ROWS, CW), 1)
    for c in range(nqk // CW):
        cs = slice(c * CW, (c + 1) * CW)
        z3 = _dot(hb, wqk_ref[:, cs]).reshape(bb, tt, CW)
        hist = hist_ref[:, :, cs]
        conv = cb_ref[:, cs] + z3 * cw_ref[CONV_W - 1:CONV_W, cs]
        for jj in range(1, CONV_W):
            rolled = pltpu.roll(z3, jj, axis=1)
            first = jnp.where(t8 < jj, pltpu.roll(hist, jj, axis=1), rolled[:, :HIST_ROWS, :])
            shifted = first if tt == HIST_ROWS else jnp.concatenate([first, rolled[:, HIST_ROWS:, :]], axis=1)
            conv = conv + shifted * cw_ref[CONV_W - 1 - jj:CONV_W - jj, cs]
        hist_ref[:, :, cs] = z3[:, tt - HIST_ROWS:, :]
        conv_ref[:, :, cs] = z3[:, tt - (CONV_W - 1):, :]
        o_ref, lo = (q_ref, c * CW) if c * CW < d else (k_ref, c * CW - d)
        o_ref[:, :, lo:lo + CW] = _silu(conv)
    for c in range(d // CW):
        cs = slice(c * CW, (c + 1) * CW)
        vm_ref[:, :, cs] = _dot(hb, wvm_ref[:, cs]).reshape(bb, tt, CW)
        gate_b = _sigmoid(_dot(hb, wo_ref[:, cs])) * _sigmoid(_dot(hb, wgb_ref[:, cs]))
        gb_ref[:, :, cs] = gate_b.reshape(bb, tt, CW)
    gif = _dot(hb, wif_ref[...])[:, :2 * MLSTM_HEADS] + bg_ref[...]
    gcol_ref[...] = gif.reshape(bb, tt, 2 * MLSTM_HEADS)


def _mix_in(x, ada, g_mix, w_in, b_gates, conv_w, conv_b, ln_g, ln_b, wtile, bias_big, conv0,
            *, bb, tt, L, ada_row0, emit_v):
    B, T, d = x.shape
    nqk = conv_w.shape[1]
    M = bb * tt
    zero_hist = conv0 is None
    tok = lambda b, t: (b, t, 0)
    row = lambda b, t: (b, 0, 0)
    in_specs = [pl.BlockSpec((bb, tt, d), tok), pl.BlockSpec((bb, N_ADA, d), _ada_rows(ada_row0, bb)),
                _resident((1, d)), _resident(w_in.shape),
                _resident((1, 2 * MLSTM_HEADS)), _resident((CONV_W, nqk)), _resident((1, nqk)),
                _resident((1, d)), _resident((1, d)), _resident((GMLP_GROUPS, M, M)),
                _resident((M, GMLP_GROUPS))]
    args = [x, ada, g_mix.reshape(1, d), w_in, b_gates.reshape(1, -1), conv_w, conv_b.reshape(1, nqk),
            ln_g.reshape(1, d), ln_b.reshape(1, d), wtile, bias_big]
    if not zero_hist:
        in_specs.append(pl.BlockSpec((bb, CONV_W - 1, nqk), row))
        args.append(conv0)
    tok_out = jax.ShapeDtypeStruct((B, T, d), _F32)
    out_shape = [tok_out] * 5 + [jax.ShapeDtypeStruct((B, T, 2 * MLSTM_HEADS), _F32),
                                 jax.ShapeDtypeStruct((B, CONV_W - 1, nqk), _F32)]
    out_specs = [pl.BlockSpec((bb, tt, d), tok)] * 5 + [
        pl.BlockSpec((bb, tt, 2 * MLSTM_HEADS), tok), pl.BlockSpec((bb, CONV_W - 1, nqk), row)]
    if emit_v:
        out_shape.append(tok_out)
        out_specs.append(pl.BlockSpec((bb, tt, d), tok))
    return pl.pallas_call(
        functools.partial(_mix_in_kernel, L=L, zero_hist=zero_hist, emit_v=emit_v),
        out_shape=out_shape,
        grid=(B // bb, T // tt),
        in_specs=in_specs,
        out_specs=out_specs,
        scratch_shapes=[pltpu.VMEM((bb, HIST_ROWS, nqk), _F32)],
        compiler_params=_params(("arbitrary", "arbitrary")),
        name="mix_in",
    )(*args)


def _mlstm_kernel(q_ref, k_ref, v_ref, gcol_ref, x_ref, ada_ref, ya_ref, gb_ref, ng_ref, wout_ref,
                  *rest, zero_state):
    if zero_state:
        o_ref, C_ref, n_ref, m_ref = rest
    else:
        C0_ref, n0_ref, m0_ref, o_ref, C_ref, n_ref, m_ref = rest
    rb, L, _ = q_ref.shape
    H, DH = MLSTM_HEADS, MLSTM_HEAD_DIM
    NB = rb * H

    @pl.when(pl.program_id(1) == 0)
    def _():
        if zero_state:
            C_ref[...] = jnp.zeros(C_ref.shape, _F32)
            n_ref[...] = jnp.zeros(n_ref.shape, _F32)
            m_ref[...] = jnp.zeros(m_ref.shape, _F32)
        else:
            C_ref[...] = C0_ref[...]
            n_ref[...] = n0_ref[...]
            m_ref[...] = m0_ref[...]

    ti = lax.broadcasted_iota(jnp.int32, (L, L), 0)
    si = lax.broadcasted_iota(jnp.int32, (L, L), 1)
    eye = ti == si
    causal = si <= ti

    chains = [(r, hd) for r in range(rb) for hd in range(H)]
    heads = lambda ref: jnp.stack([ref[r, :, hd * DH:(hd + 1) * DH] for r, hd in chains])
    qh = heads(q_ref)
    kh = heads(k_ref) * (DH ** -0.5)
    qb = qh.astype(_BF16)
    kb = kh.astype(_BF16)
    vb = heads(v_ref).astype(_BF16)
    Ch = C_ref[...].reshape(NB, DH, DH)
    nh = n_ref[...].reshape(NB, 1, DH)
    m_old = m_ref[...].reshape(NB, 1, GATE_PAD)[:, :, 0:1]
    gates = gcol_ref[...]
    i_col = jnp.stack([gates[r, :, hd:hd + 1] for r, hd in chains])
    lf_col = _log_sigmoid(jnp.stack([gates[r, :, H + hd:H + hd + 1] for r, hd in chains]))

    lf_row = jnp.sum(jnp.where(eye, lf_col, 0.0), axis=1, keepdims=True)
    b_col = jnp.sum(jnp.where(causal, lf_row, 0.0), axis=2, keepdims=True)
    c_col = i_col - b_col
    r_row = jnp.sum(jnp.where(eye, c_col, 0.0), axis=1, keepdims=True)
    b_last = b_col[:, L - 1:L, :]

    inter = b_col + m_old
    dm = jnp.where(causal, b_col + r_row, -jnp.inf)
    m_t = jnp.maximum(inter, jnp.max(dm, axis=2, keepdims=True))
    w_inter = jnp.exp(inter - m_t)
    s = _bdot(qb, kb, 2, 2) * jnp.exp(dm - m_t)
    num = w_inter * _bdot(qb, Ch.astype(_BF16), 2, 1) + _bdot(s.astype(_BF16), vb, 2, 1)
    den = (w_inter * jnp.sum(qh * nh, axis=2, keepdims=True)
           + jnp.sum(s, axis=2, keepdims=True))
    hh = num * (1.0 / jnp.maximum(jnp.abs(den), jnp.exp(-m_t)))

    m_new = jnp.maximum(b_last + m_old, jnp.max(b_last + r_row, axis=2, keepdims=True))
    a_prev = jnp.exp(b_last + m_old - m_new)
    kw = kh * jnp.exp(b_last + c_col - m_new)
    C_new = a_prev * Ch + _bdot(kw.astype(_BF16), vb, 1, 1)
    C_ref[...] = C_new.reshape(rb, H, DH, DH)
    n_ref[...] = (a_prev * nh + jnp.sum(kw, axis=1, keepdims=True)).reshape(rb, H, DH)
    m_ref[...] = jnp.broadcast_to(m_new, (NB, 1, GATE_PAD)).reshape(rb, H, GATE_PAD)

    mu = jnp.mean(hh, axis=2, keepdims=True)
    var = jnp.mean(jnp.square(hh - mu), axis=2, keepdims=True)
    ng = jnp.stack([ng_ref[:, hd * DH:(hd + 1) * DH] for _, hd in chains])
    hn = (hh - mu) * lax.rsqrt(var + EPS) * ng

    d = x_ref.shape[2]
    mix = jnp.concatenate(
        [jnp.concatenate([ya_ref[r, :, hd * DH:(hd + 1) * DH]
                          + gb_ref[r, :, hd * DH:(hd + 1) * DH] * hn[r * H + hd] for hd in range(H)], axis=1)
         for r in range(rb)], axis=0)
    out = _dot(mix.astype(_BF16), wout_ref[...]).reshape(rb, L, d)
    o_ref[...] = x_ref[...] + ada_ref[:, 5:6, :] * out


def _mlstm(q, k, v, gcol, x, ada, ya, gb, norm_g, w_out, state, *, L, rb, ada_row0):
    B, T, d = q.shape
    H, DH = MLSTM_HEADS, MLSTM_HEAD_DIM
    zero_state = state is None
    tok = lambda b, c: (b, c, 0)
    blk = pl.BlockSpec((rb, L, d), tok)
    in_specs = [blk] * 3 + [pl.BlockSpec((rb, L, 2 * H), tok), blk,
                            pl.BlockSpec((rb, N_ADA, d), _ada_rows(ada_row0, rb)), blk, blk,
                            _resident((1, d)), _resident((d, d))]
    args = [q, k, v, gcol, x, ada, ya, gb, norm_g.reshape(1, d), w_out]
    st_specs = [pl.BlockSpec((rb, H, DH, DH), lambda b, c: (b, 0, 0, 0)),
                pl.BlockSpec((rb, H, DH), lambda b, c: (b, 0, 0)),
                pl.BlockSpec((rb, H, GATE_PAD), lambda b, c: (b, 0, 0))]
    if not zero_state:
        in_specs += st_specs
        args += list(state)
    return pl.pallas_call(
        functools.partial(_mlstm_kernel, zero_state=zero_state),
        out_shape=[jax.ShapeDtypeStruct((B, T, d), _F32),
                   jax.ShapeDtypeStruct((B, H, DH, DH), _F32),
                   jax.ShapeDtypeStruct((B, H, DH), _F32),
                   jax.ShapeDtypeStruct((B, H, GATE_PAD), _F32)],
        grid=(B // rb, T // L),
        in_specs=in_specs,
        out_specs=[pl.BlockSpec((rb, L, d), tok)] + st_specs,
        compiler_params=_params(("arbitrary", "arbitrary")),
        name="mlstm",
    )(*args)


def _trunk(x, ada, fin, conv0, state, W, *, ada_row0, bb, tt, tt_ffn, rb, emit_v):
    B, T, d = x.shape
    L = min(T, GMLP_CHUNK)
    assert L == math.gcd(T, MLSTM_CHUNK) and tt % L == 0 and tt % 8 == 0
    M = bb * tt
    wrow = jnp.tile(W["gmlp_ws"][:, :L, :L], (1, 1, M // L))
    wtile = jnp.broadcast_to(wrow[:, None], (GMLP_GROUPS, M // L, L, M)).reshape(GMLP_GROUPS, M, M)
    bias_big = jnp.tile(W["gmlp_bs"][:, :L].T, (M // L, 1))

    x = _ffn(x, ada, W["g_ffn1"], *W["ffn1"], j=0, bb=bb, tt=tt_ffn, ada_row0=ada_row0)
    outs = _mix_in(x, ada, W["g_mix"], W["w_in"], W["b_gates"], W["conv_w"], W["conv_b"],
                   W["gmlp_ln_g"], W["gmlp_ln_b"], wtile, bias_big, conv0,
                   bb=bb, tt=tt, L=L, ada_row0=ada_row0, emit_v=emit_v)
    ya, gb, q, k, vm, gcol, conv_new = outs[:7]
    x, C, n, m = _mlstm(q, k, vm, gcol, x, ada, ya, gb, W["mlstm_norm_g"], W["w_out"], state,
                        L=L, rb=rb, ada_row0=ada_row0)
    y = _ffn(x, ada, W["g_ffn2"], *W["ffn2"], j=6, bb=bb, tt=tt_ffn, ada_row0=ada_row0,
             fin=fin, g_final=W["g_final"])
    v = outs[7] if emit_v else None
    return y, conv_new[None], C[None], n[None], m[None, :, :, 0], v


def kernel(x_prompt, x_sample, c_prompt, c_sample, state_mlstm_C, state_mlstm_n, state_mlstm_m, state_conv, w_ada, b_ada, g_ffn1, w_ffn1_in, w_ffn1_out, g_mix, w_in, b_gates, conv_w, conv_b, gmlp_ln_g, gmlp_ln_b, gmlp_ws, gmlp_bs, mlstm_norm_g, w_out, g_ffn2, w_ffn2_in, w_ffn2_out, w_ada_final, b_ada_final, g_final):
    assert w_ada.shape[0] == 1, "single-layer trunk only"
    d, H = D_MODEL, MLSTM_HEADS
    bf = lambda a: a.astype(_BF16)

    W = {
        "ffn1": (bf(w_ffn1_in[0]), bf(w_ffn1_out[0])), "ffn2": (bf(w_ffn2_in[0]), bf(w_ffn2_out[0])),
        "w_in": _prep_w_in(w_in[0], 128),
        "w_out": bf(w_out[0]),
        "g_ffn1": g_ffn1[0], "g_mix": g_mix[0], "g_ffn2": g_ffn2[0], "g_final": g_final,
        "b_gates": b_gates[0], "conv_w": conv_w[0], "conv_b": conv_b[0],
        "gmlp_ln_g": gmlp_ln_g[0], "gmlp_ln_b": gmlp_ln_b[0],
        "gmlp_ws": gmlp_ws[0], "gmlp_bs": gmlp_bs[0], "mlstm_norm_g": mlstm_norm_g[0],
    }

    Bs = x_sample.shape[0]
    c_all = jnp.concatenate([c_sample, c_prompt], axis=0)
    ada_all = _ada(c_all, w_ada[0], b_ada[0], 1024).reshape(-1, N_ADA, d)
    fin_all = _ada(c_all, w_ada_final, b_ada_final, 1024).reshape(-1, 2, d)

    y_p, conv_p, C_p, n_p, m_p, _ = _trunk(
        x_prompt, ada_all, fin_all, None, None, W,
        ada_row0=Bs, bb=1, tt=512, tt_ffn=512, rb=4, emit_v=False)
    m0 = jnp.broadcast_to(state_mlstm_m[0][:, :, None], state_mlstm_m.shape[1:] + (GATE_PAD,))
    y_s, conv_s, C_s, n_s, m_s, v_s = _trunk(
        x_sample, ada_all, fin_all, state_conv[0],
        (state_mlstm_C[0], state_mlstm_n[0], m0), W,
        ada_row0=0, bb=32, tt=8, tt_ffn=8, rb=8, emit_v=True)
    return (y_p, y_s, C_p, n_p, m_p, conv_p, C_s, n_s, m_s, conv_s, v_s[None])
```

```python
import functools
import math

import jax
import jax.numpy as jnp
from jax import lax
from jax.experimental import pallas as pl
from jax.experimental.pallas import tpu as pltpu

D_MODEL = 1024
N_ADA = 9
GMLP_GROUPS = 4
GMLP_GROUP_DIM = D_MODEL // GMLP_GROUPS
GMLP_CHUNK = 128
MLSTM_HEADS = 4
MLSTM_HEAD_DIM = D_MODEL // MLSTM_HEADS
MLSTM_CHUNK = 128
CONV_W = 4
EPS = 1e-6
GATE_PAD = 128
HIST_ROWS = 8
VMEM_LIMIT_BYTES = 56 * 1024 * 1024

_BF16 = jnp.bfloat16
_F32 = jnp.float32


def _dot(a, b):
    return jnp.dot(a, b, preferred_element_type=_F32)


def _dot_nt(a, b):
    return lax.dot_general(a, b, (((1,), (1,)), ((), ())), preferred_element_type=_F32)


def _bdot(a, b, ca, cb):
    return lax.dot_general(a, b, (((ca,), (cb,)), ((0,), (0,))), preferred_element_type=_F32)


def _sigmoid(x):
    return 1.0 / (1.0 + jnp.exp(-x))


def _silu(x):
    return x * _sigmoid(x)


def _gelu_tanh(x):
    return 0.5 * x * (1.0 + jnp.tanh(0.7978845608028654 * (x + 0.044715 * (x * x * x))))


def _log_sigmoid(x):
    return jnp.minimum(x, 0.0) - jnp.log1p(jnp.exp(-jnp.abs(x)))


def _rms_mod(x, g, shift, scale):
    y = x * lax.rsqrt(jnp.mean(x * x, axis=-1, keepdims=True) + EPS) * g
    return y * (1.0 + scale) + shift


def _resident(shape):
    nd = len(shape)
    return pl.BlockSpec(shape, lambda *_: (0,) * nd, pipeline_mode=pl.Buffered(1))


def _params(semantics):
    return pltpu.CompilerParams(dimension_semantics=semantics, vmem_limit_bytes=VMEM_LIMIT_BYTES)


def _ada_kernel(c_ref, w_ref, b_ref, o_ref):
    cs = _silu(c_ref[...]).astype(_BF16)
    o_ref[...] = _dot(cs, w_ref[...].astype(_BF16)) + b_ref[...]


def _ada(c, w, b, bn):
    bc, d = c.shape
    n = w.shape[1]
    return pl.pallas_call(
        _ada_kernel,
        out_shape=jax.ShapeDtypeStruct((bc, n), _F32),
        grid=(n // bn,),
        in_specs=[pl.BlockSpec((bc, d), lambda j: (0, 0)),
                  pl.BlockSpec((d, bn), lambda j: (0, j)),
                  pl.BlockSpec((1, bn), lambda j: (0, j))],
        out_specs=pl.BlockSpec((bc, bn), lambda j: (0, j)),
        compiler_params=_params(("arbitrary",)),
        name="ada",
    )(c, w, b.reshape(1, n))


def _ffn_kernel(x_ref, ada_ref, g_ref, wi_ref, wo_ref, *rest, j, final):
    if final:
        fin_ref, gf_ref, o_ref = rest
    else:
        (o_ref,) = rest
    bb, tt, d = x_ref.shape
    dff = wo_ref.shape[0]
    x = x_ref[...]
    h = _rms_mod(x, g_ref[...], ada_ref[:, j:j + 1, :], ada_ref[:, j + 1:j + 2, :])
    hb = h.reshape(bb * tt, d).astype(_BF16)
    a = _dot(hb, wi_ref[:, :dff])
    b = _dot(hb, wi_ref[:, dff:])
    gated = (_silu(a) * b).astype(_BF16)
    out = _dot(gated, wo_ref[...]).reshape(bb, tt, d)
    y = x + 0.5 * ada_ref[:, j + 2:j + 3, :] * out
    if final:
        y = _rms_mod(y, gf_ref[...], fin_ref[:, 0:1, :], fin_ref[:, 1:2, :])
    o_ref[...] = y


def _ada_rows(ada_row0, bb):
    assert ada_row0 % bb == 0
    return lambda b, t: (b + ada_row0 // bb, 0, 0)


def _ffn(x, ada, g, wi, wo, *, j, bb, tt, ada_row0, fin=None, g_final=None):
    B, T, d = x.shape
    dff = wo.shape[0]
    final = fin is not None
    tok = lambda b, t: (b, t, 0)
    row = _ada_rows(ada_row0, bb)
    in_specs = [pl.BlockSpec((bb, tt, d), tok),
                pl.BlockSpec((bb, N_ADA, d), row),
                _resident((1, d)), _resident((d, 2 * dff)), _resident((dff, d))]
    args = [x, ada, g.reshape(1, d), wi, wo]
    if final:
        in_specs += [pl.BlockSpec((bb, 2, d), row), _resident((1, d))]
        args += [fin, g_final.reshape(1, d)]
    return pl.pallas_call(
        functools.partial(_ffn_kernel, j=j, final=final),
        out_shape=jax.ShapeDtypeStruct((B, T, d), _F32),
        grid=(B // bb, T // tt),
        in_specs=in_specs,
        out_specs=pl.BlockSpec((bb, tt, d), tok),
        compiler_params=_params(("arbitrary", "arbitrary")),
        name="ffn_final" if final else "ffn",
    )(*args)


PREP_COLS = 256


def _prep_w_in_kernel(a_ref, b_ref, o_ref, *, n_main, n_tail):
    j = pl.program_id(0)
    blk, d = a_ref.shape
    ng = 2 * MLSTM_HEADS

    @pl.when(j < n_main)
    def _():
        o_ref[...] = a_ref[...].T.astype(_BF16)

    @pl.when(jnp.logical_and(j >= n_main, j < n_main + n_tail))
    def _():
        rows = jnp.concatenate([a_ref[ng:, :], b_ref[:ng, :]], axis=0)
        o_ref[...] = rows.T.astype(_BF16)

    @pl.when(j == n_main + n_tail)
    def _():
        rows = jnp.concatenate([a_ref[:ng, :], jnp.zeros((blk - ng, d), _F32)], axis=0)
        o_ref[...] = rows.T.astype(_BF16)


def _prep_w_in(w_t):
    n_in, d = w_t.shape
    blk = PREP_COLS
    n_main = 6 * D_MODEL // blk
    n_tail = 2 * D_MODEL // blk
    assert n_in == (n_main + n_tail) * blk + 2 * MLSTM_HEADS
    steps = n_main + n_tail + 1
    a_map = lambda j: (jnp.where(j < steps - 1, j, n_main), 0)
    b_map = lambda j: (jnp.clip(j + 1, n_main, n_main + n_tail), 0)
    return pl.pallas_call(
        functools.partial(_prep_w_in_kernel, n_main=n_main, n_tail=n_tail),
        out_shape=jax.ShapeDtypeStruct((d, steps * blk), _BF16),
        grid=(steps,),
        in_specs=[pl.BlockSpec((blk, d), a_map), pl.BlockSpec((blk, d), b_map)],
        out_specs=pl.BlockSpec((d, blk), lambda j: (0, j)),
        compiler_params=_params(("arbitrary",)),
        name="prep_w_in",
    )(w_t, w_t)


def _mix_in_kernel(x_ref, ada_ref, g_ref, w_ref, bg_ref, cw_ref, cb_ref, lng_ref, lnb_ref, wt_ref,
                   bias_ref, *rest, L, zero_hist, emit_v):
    dm = x_ref.shape[2]
    edges = [0, dm, 2 * dm, 4 * dm, 5 * dm, 6 * dm, 7 * dm, 8 * dm, 8 * dm + GATE_PAD]
    wu_ref, wv_ref, wqk_ref, wvm_ref, wo_ref, wga_ref, wgb_ref, wif_ref = [
        w_ref.at[:, lo:hi] for lo, hi in zip(edges[:-1], edges[1:])]
    rest = list(rest)
    conv0_ref = None if zero_hist else rest.pop(0)
    ya_ref, gb_ref, q_ref, k_ref, vm_ref, gcol_ref, conv_ref = rest[:7]
    rest = rest[7:]
    v_ref = rest.pop(0) if emit_v else None
    (hist_ref,) = rest

    bb, tt, d = x_ref.shape
    M = bb * tt
    t_idx = pl.program_id(1)

    h = _rms_mod(x_ref[...], g_ref[...], ada_ref[:, 3:4, :], ada_ref[:, 4:5, :])
    hb = h.reshape(M, d).astype(_BF16)

    gv = _gelu_tanh(_dot(hb, wv_ref[...]))
    mu = jnp.mean(gv, axis=-1, keepdims=True)
    var = jnp.mean(jnp.square(gv - mu), axis=-1, keepdims=True)
    v = (gv - mu) * lax.rsqrt(var + EPS) * lng_ref[...] + lnb_ref[...]
    if emit_v:
        v_ref[...] = v.reshape(bb, tt, d)
    vb = v.astype(_BF16)
    rows = lax.broadcasted_iota(jnp.int32, (M, M), 0)
    cols = lax.broadcasted_iota(jnp.int32, (M, M), 1)
    mask = jnp.logical_and(rows // L == cols // L, cols <= rows)
    CW = GMLP_GROUP_DIM
    for g in range(GMLP_GROUPS):
        sl = slice(g * CW, (g + 1) * CW)
        wg = jnp.where(mask, wt_ref[g], 0.0).astype(_BF16)
        mixed = _dot(wg, vb[:, sl]) + bias_ref[:, g:g + 1]
        u_g = _gelu_tanh(_dot(hb, wu_ref[:, sl]))
        s_g = _sigmoid(_dot(hb, wga_ref[:, sl]))
        ya_ref[:, :, sl] = (s_g * u_g * mixed).reshape(bb, tt, CW)

    nqk = cw_ref.shape[1]

    @pl.when(t_idx == 0)
    def _():
        hist_ref[...] = jnp.zeros((bb, HIST_ROWS, nqk), _F32)
        if not zero_hist:
            hist_ref[:, HIST_ROWS - (CONV_W - 1):HIST_ROWS, :] = conv0_ref[...]

    t8 = lax.broadcasted_iota(jnp.int32, (bb, HIST_ROWS, CW), 1)
    for c in range(nqk // CW):
        cs = slice(c * CW, (c + 1) * CW)
        z3 = _dot(hb, wqk_ref[:, cs]).reshape(bb, tt, CW)
        hist = hist_ref[:, :, cs]
        conv = cb_ref[:, cs] + z3 * cw_ref[CONV_W - 1:CONV_W, cs]
        for jj in range(1, CONV_W):
            rolled = pltpu.roll(z3, jj, axis=1)
            first = jnp.where(t8 < jj, pltpu.roll(hist, jj, axis=1), rolled[:, :HIST_ROWS, :])
            shifted = first if tt == HIST_ROWS else jnp.concatenate([first, rolled[:, HIST_ROWS:, :]], axis=1)
            conv = conv + shifted * cw_ref[CONV_W - 1 - jj:CONV_W - jj, cs]
        hist_ref[:, :, cs] = z3[:, tt - HIST_ROWS:, :]
        conv_ref[:, :, cs] = z3[:, tt - (CONV_W - 1):, :]
        o_ref, lo = (q_ref, c * CW) if c * CW < d else (k_ref, c * CW - d)
        o_ref[:, :, lo:lo + CW] = _silu(conv)
    for c in range(d // CW):
        cs = slice(c * CW, (c + 1) * CW)
        vm_ref[:, :, cs] = _dot(hb, wvm_ref[:, cs]).reshape(bb, tt, CW)
        gate_b = _sigmoid(_dot(hb, wo_ref[:, cs])) * _sigmoid(_dot(hb, wgb_ref[:, cs]))
        gb_ref[:, :, cs] = gate_b.reshape(bb, tt, CW)
    gif = _dot(hb, wif_ref[...])[:, :2 * MLSTM_HEADS] + bg_ref[...]
    gcol_ref[...] = gif.reshape(bb, tt, 2 * MLSTM_HEADS)


def _mix_in(x, ada, g_mix, w_in, b_gates, conv_w, conv_b, ln_g, ln_b, wtile, bias_big, conv0,
            *, bb, tt, L, ada_row0, emit_v):
    B, T, d = x.shape
    nqk = conv_w.shape[1]
    M = bb * tt
    zero_hist = conv0 is None
    tok = lambda b, t: (b, t, 0)
    row = lambda b, t: (b, 0, 0)
    in_specs = [pl.BlockSpec((bb, tt, d), tok), pl.BlockSpec((bb, N_ADA, d), _ada_rows(ada_row0, bb)),
                _resident((1, d)), _resident(w_in.shape),
                _resident((1, 2 * MLSTM_HEADS)), _resident((CONV_W, nqk)), _resident((1, nqk)),
                _resident((1, d)), _resident((1, d)), _resident((GMLP_GROUPS, M, M)),
                _resident((M, GMLP_GROUPS))]
    args = [x, ada, g_mix.reshape(1, d), w_in, b_gates.reshape(1, -1), conv_w, conv_b.reshape(1, nqk),
            ln_g.reshape(1, d), ln_b.reshape(1, d), wtile, bias_big]
    if not zero_hist:
        in_specs.append(pl.BlockSpec((bb, CONV_W - 1, nqk), row))
        args.append(conv0)
    tok_out = jax.ShapeDtypeStruct((B, T, d), _F32)
    out_shape = [tok_out] * 5 + [jax.ShapeDtypeStruct((B, T, 2 * MLSTM_HEADS), _F32),
                                 jax.ShapeDtypeStruct((B, CONV_W - 1, nqk), _F32)]
    out_specs = [pl.BlockSpec((bb, tt, d), tok)] * 5 + [
        pl.BlockSpec((bb, tt, 2 * MLSTM_HEADS), tok), pl.BlockSpec((bb, CONV_W - 1, nqk), row)]
    if emit_v:
        out_shape.append(tok_out)
        out_specs.append(pl.BlockSpec((bb, tt, d), tok))
    return pl.pallas_call(
        functools.partial(_mix_in_kernel, L=L, zero_hist=zero_hist, emit_v=emit_v),
        out_shape=out_shape,
        grid=(B // bb, T // tt),
        in_specs=in_specs,
        out_specs=out_specs,
        scratch_shapes=[pltpu.VMEM((bb, HIST_ROWS, nqk), _F32)],
        compiler_params=_params(("arbitrary", "arbitrary")),
        name="mix_in",
    )(*args)


def _mlstm_kernel(q_ref, k_ref, v_ref, gcol_ref, x_ref, ada_ref, ya_ref, gb_ref, ng_ref, wout_ref,
                  *rest, zero_state):
    if zero_state:
        o_ref, C_ref, n_ref, m_ref = rest
    else:
        C0_ref, n0_ref, m0_ref, o_ref, C_ref, n_ref, m_ref = rest
    rb, L, _ = q_ref.shape
    H, DH = MLSTM_HEADS, MLSTM_HEAD_DIM
    NB = rb * H

    @pl.when(pl.program_id(1) == 0)
    def _():
        if zero_state:
            C_ref[...] = jnp.zeros(C_ref.shape, _F32)
            n_ref[...] = jnp.zeros(n_ref.shape, _F32)
            m_ref[...] = jnp.zeros(m_ref.shape, _F32)
        else:
            C_ref[...] = C0_ref[...]
            n_ref[...] = n0_ref[...]
            m_ref[...] = m0_ref[...]

    ti = lax.broadcasted_iota(jnp.int32, (L, L), 0)
    si = lax.broadcasted_iota(jnp.int32, (L, L), 1)
    eye = ti == si
    causal = si <= ti

    chains = [(r, hd) for r in range(rb) for hd in range(H)]
    heads = lambda ref: jnp.stack([ref[r, :, hd * DH:(hd + 1) * DH] for r, hd in chains])
    qh = heads(q_ref)
    kh = heads(k_ref) * (DH ** -0.5)
    qb = qh.astype(_BF16)
    kb = kh.astype(_BF16)
    vb = heads(v_ref).astype(_BF16)
    Ch = C_ref[...].reshape(NB, DH, DH)
    nh = n_ref[...].reshape(NB, 1, DH)
    m_old = m_ref[...].reshape(NB, 1, GATE_PAD)[:, :, 0:1]
    gates = gcol_ref[...]
    i_col = jnp.stack([gates[r, :, hd:hd + 1] for r, hd in chains])
    lf_col = _log_sigmoid(jnp.stack([gates[r, :, H + hd:H + hd + 1] for r, hd in chains]))

    lf_row = jnp.sum(jnp.where(eye, lf_col, 0.0), axis=1, keepdims=True)
    b_col = jnp.sum(jnp.where(causal, lf_row, 0.0), axis=2, keepdims=True)
    c_col = i_col - b_col
    r_row = jnp.sum(jnp.where(eye, c_col, 0.0), axis=1, keepdims=True)
    b_last = b_col[:, L - 1:L, :]

    inter = b_col + m_old
    dm = jnp.where(causal, b_col + r_row, -jnp.inf)
    m_t = jnp.maximum(inter, jnp.max(dm, axis=2, keepdims=True))
    w_inter = jnp.exp(inter - m_t)
    s = _bdot(qb, kb, 2, 2) * jnp.exp(dm - m_t)
    num = w_inter * _bdot(qb, Ch.astype(_BF16), 2, 1) + _bdot(s.astype(_BF16), vb, 2, 1)
    den = (w_inter * jnp.sum(qh * nh, axis=2, keepdims=True)
           + jnp.sum(s, axis=2, keepdims=True))
    hh = num * (1.0 / jnp.maximum(jnp.abs(den), jnp.exp(-m_t)))

    m_new = jnp.maximum(b_last + m_old, jnp.max(b_last + r_row, axis=2, keepdims=True))
    a_prev = jnp.exp(b_last + m_old - m_new)
    kw = kh * jnp.exp(b_last + c_col - m_new)
    C_new = a_prev * Ch + _bdot(kw.astype(_BF16), vb, 1, 1)
    C_ref[...] = C_new.reshape(rb, H, DH, DH)
    n_ref[...] = (a_prev * nh + jnp.sum(kw, axis=1, keepdims=True)).reshape(rb, H, DH)
    m_ref[...] = jnp.broadcast_to(m_new, (NB, 1, GATE_PAD)).reshape(rb, H, GATE_PAD)

    mu = jnp.mean(hh, axis=2, keepdims=True)
    var = jnp.mean(jnp.square(hh - mu), axis=2, keepdims=True)
    ng = jnp.stack([ng_ref[:, hd * DH:(hd + 1) * DH] for _, hd in chains])
    hn = (hh - mu) * lax.rsqrt(var + EPS) * ng

    d = x_ref.shape[2]
    mix = jnp.concatenate(
        [jnp.concatenate([ya_ref[r, :, hd * DH:(hd + 1) * DH]
                          + gb_ref[r, :, hd * DH:(hd + 1) * DH] * hn[r * H + hd] for hd in range(H)], axis=1)
         for r in range(rb)], axis=0)
    out = _dot(mix.astype(_BF16), wout_ref[...]).reshape(rb, L, d)
    o_ref[...] = x_ref[...] + ada_ref[:, 5:6, :] * out


def _mlstm(q, k, v, gcol, x, ada, ya, gb, norm_g, w_out, state, *, L, rb, ada_row0):
    B, T, d = q.shape
    H, DH = MLSTM_HEADS, MLSTM_HEAD_DIM
    zero_state = state is None
    tok = lambda b, c: (b, c, 0)
    blk = pl.BlockSpec((rb, L, d), tok)
    in_specs = [blk] * 3 + [pl.BlockSpec((rb, L, 2 * H), tok), blk,
                            pl.BlockSpec((rb, N_ADA, d), _ada_rows(ada_row0, rb)), blk, blk,
                            _resident((1, d)), _resident((d, d))]
    args = [q, k, v, gcol, x, ada, ya, gb, norm_g.reshape(1, d), w_out]
    st_specs = [pl.BlockSpec((rb, H, DH, DH), lambda b, c: (b, 0, 0, 0)),
                pl.BlockSpec((rb, H, DH), lambda b, c: (b, 0, 0)),
                pl.BlockSpec((rb, H, GATE_PAD), lambda b, c: (b, 0, 0))]
    if not zero_state:
        in_specs += st_specs
        args += list(state)
    return pl.pallas_call(
        functools.partial(_mlstm_kernel, zero_state=zero_state),
        out_shape=[jax.ShapeDtypeStruct((B, T, d), _F32),
                   jax.ShapeDtypeStruct((B, H, DH, DH), _F32),
                   jax.ShapeDtypeStruct((B, H, DH), _F32),
                   jax.ShapeDtypeStruct((B, H, GATE_PAD), _F32)],
        grid=(B // rb, T // L),
        in_specs=in_specs,
        out_specs=[pl.BlockSpec((rb, L, d), tok)] + st_specs,
        compiler_params=_params(("arbitrary", "arbitrary")),
        name="mlstm",
    )(*args)


def _trunk(x, ada, fin, conv0, state, W, *, ada_row0, bb, tt, tt_ffn, rb, emit_v):
    B, T, d = x.shape
    L = min(T, GMLP_CHUNK)
    assert L == math.gcd(T, MLSTM_CHUNK) and tt % L == 0 and tt % 8 == 0
    M = bb * tt
    wrow = jnp.tile(W["gmlp_ws"][:, :L, :L], (1, 1, M // L))
    wtile = jnp.broadcast_to(wrow[:, None], (GMLP_GROUPS, M // L, L, M)).reshape(GMLP_GROUPS, M, M)
    bias_big = jnp.tile(W["gmlp_bs"][:, :L].T, (M // L, 1))

    x = _ffn(x, ada, W["g_ffn1"], *W["ffn1"], j=0, bb=bb, tt=tt_ffn, ada_row0=ada_row0)
    outs = _mix_in(x, ada, W["g_mix"], W["w_in"], W["b_gates"], W["conv_w"], W["conv_b"],
                   W["gmlp_ln_g"], W["gmlp_ln_b"], wtile, bias_big, conv0,
                   bb=bb, tt=tt, L=L, ada_row0=ada_row0, emit_v=emit_v)
    ya, gb, q, k, vm, gcol, conv_new = outs[:7]
    x, C, n, m = _mlstm(q, k, vm, gcol, x, ada, ya, gb, W["mlstm_norm_g"], W["w_out"], state,
                        L=L, rb=rb, ada_row0=ada_row0)
    y = _ffn(x, ada, W["g_ffn2"], *W["ffn2"], j=6, bb=bb, tt=tt_ffn, ada_row0=ada_row0,
             fin=fin, g_final=W["g_final"])
    v = outs[7] if emit_v else None
    return y, conv_new[None], C[None], n[None], m[None, :, :, 0], v


def kernel(x_prompt, x_sample, c_prompt, c_sample, state_mlstm_C, state_mlstm_n, state_mlstm_m, state_conv, w_ada, b_ada, g_ffn1, w_ffn1_in, w_ffn1_out, g_mix, w_in, b_gates, conv_w, conv_b, gmlp_ln_g, gmlp_ln_b, gmlp_ws, gmlp_bs, mlstm_norm_g, w_out, g_ffn2, w_ffn2_in, w_ffn2_out, w_ada_final, b_ada_final, g_final):
    assert w_ada.shape[0] == 1, "single-layer trunk only"
    d, H = D_MODEL, MLSTM_HEADS
    bf = lambda a: a.astype(_BF16)

    W = {
        "ffn1": (bf(w_ffn1_in[0]), bf(w_ffn1_out[0])), "ffn2": (bf(w_ffn2_in[0]), bf(w_ffn2_out[0])),
        "w_in": _prep_w_in(w_in[0].T),
        "w_out": bf(w_out[0]),
        "g_ffn1": g_ffn1[0], "g_mix": g_mix[0], "g_ffn2": g_ffn2[0], "g_final": g_final,
        "b_gates": b_gates[0], "conv_w": conv_w[0], "conv_b": conv_b[0],
        "gmlp_ln_g": gmlp_ln_g[0], "gmlp_ln_b": gmlp_ln_b[0],
        "gmlp_ws": gmlp_ws[0], "gmlp_bs": gmlp_bs[0], "mlstm_norm_g": mlstm_norm_g[0],
    }

    Bs = x_sample.shape[0]
    c_all = jnp.concatenate([c_sample, c_prompt], axis=0)
    ada_all = _ada(c_all, w_ada[0], b_ada[0], 1024).reshape(-1, N_ADA, d)
    fin_all = _ada(c_all, w_ada_final, b_ada_final, 1024).reshape(-1, 2, d)

    y_p, conv_p, C_p, n_p, m_p, _ = _trunk(
        x_prompt, ada_all, fin_all, None, None, W,
        ada_row0=Bs, bb=1, tt=512, tt_ffn=512, rb=4, emit_v=False)
    m0 = jnp.broadcast_to(state_mlstm_m[0][:, :, None], state_mlstm_m.shape[1:] + (GATE_PAD,))
    y_s, conv_s, C_s, n_s, m_s, v_s = _trunk(
        x_sample, ada_all, fin_all, state_conv[0],
        (state_mlstm_C[0], state_mlstm_n[0], m0), W,
        ada_row0=0, bb=32, tt=8, tt_ffn=8, rb=8, emit_v=True)
    return (y_p, y_s, C_p, n_p, m_p, conv_p, C_s, n_s, m_s, conv_s, v_s[None])
```

```python
import functools
import math

import jax
import jax.numpy as jnp
from jax import lax
from jax.experimental import pallas as pl
from jax.experimental.pallas import tpu as pltpu

D_MODEL = 1024
N_ADA = 9
GMLP_GROUPS = 4
GMLP_GROUP_DIM = D_MODEL // GMLP_GROUPS
GMLP_CHUNK = 128
MLSTM_HEADS = 4
MLSTM_HEAD_DIM = D_MODEL // MLSTM_HEADS
MLSTM_CHUNK = 128
CONV_W = 4
EPS = 1e-6
GATE_PAD = 128
HIST_ROWS = 8
VMEM_LIMIT_BYTES = 56 * 1024 * 1024

_BF16 = jnp.bfloat16
_F32 = jnp.float32


def _dot(a, b):
    return jnp.dot(a, b, preferred_element_type=_F32)


def _dot_nt(a, b):
    return lax.dot_general(a, b, (((1,), (1,)), ((), ())), preferred_element_type=_F32)


def _bdot(a, b, ca, cb):
    return lax.dot_general(a, b, (((ca,), (cb,)), ((0,), (0,))), preferred_element_type=_F32)


def _sigmoid(x):
    return 1.0 / (1.0 + jnp.exp(-x))


def _silu(x):
    return x * _sigmoid(x)


def _gelu_tanh(x):
    return 0.5 * x * (1.0 + jnp.tanh(0.7978845608028654 * (x + 0.044715 * (x * x * x))))


def _log_sigmoid(x):
    return jnp.minimum(x, 0.0) - jnp.log1p(jnp.exp(-jnp.abs(x)))


def _rms_mod(x, g, shift, scale):
    y = x * lax.rsqrt(jnp.mean(x * x, axis=-1, keepdims=True) + EPS) * g
    return y * (1.0 + scale) + shift


def _resident(shape):
    nd = len(shape)
    return pl.BlockSpec(shape, lambda *_: (0,) * nd, pipeline_mode=pl.Buffered(1))


def _params(semantics):
    return pltpu.CompilerParams(dimension_semantics=semantics, vmem_limit_bytes=VMEM_LIMIT_BYTES)


def _ada_kernel(c_ref, w_ref, b_ref, o_ref):
    cs = _silu(c_ref[...]).astype(_BF16)
    o_ref[...] = _dot(cs, w_ref[...].astype(_BF16)) + b_ref[...]


def _ada(c, w, b, bn):
    bc, d = c.shape
    n = w.shape[1]
    return pl.pallas_call(
        _ada_kernel,
        out_shape=jax.ShapeDtypeStruct((bc, n), _F32),
        grid=(n // bn,),
        in_specs=[pl.BlockSpec((bc, d), lambda j: (0, 0)),
                  pl.BlockSpec((d, bn), lambda j: (0, j)),
                  pl.BlockSpec((1, bn), lambda j: (0, j))],
        out_specs=pl.BlockSpec((bc, bn), lambda j: (0, j)),
        compiler_params=_params(("arbitrary",)),
        name="ada",
    )(c, w, b.reshape(1, n))


def _ffn_kernel(x_ref, ada_ref, g_ref, wi_ref, wo_ref, *rest, j, final):
    if final:
        fin_ref, gf_ref, o_ref = rest
    else:
        (o_ref,) = rest
    bb, tt, d = x_ref.shape
    dff = wo_ref.shape[0]
    x = x_ref[...]
    h = _rms_mod(x, g_ref[...], ada_ref[:, j:j + 1, :], ada_ref[:, j + 1:j + 2, :])
    hb = h.reshape(bb * tt, d).astype(_BF16)
    a = _dot(hb, wi_ref[:, :dff])
    b = _dot(hb, wi_ref[:, dff:])
    gated = (_silu(a) * b).astype(_BF16)
    out = _dot(gated, wo_ref[...]).reshape(bb, tt, d)
    y = x + 0.5 * ada_ref[:, j + 2:j + 3, :] * out
    if final:
        y = _rms_mod(y, gf_ref[...], fin_ref[:, 0:1, :], fin_ref[:, 1:2, :])
    o_ref[...] = y


def _ada_rows(ada_row0, bb):
    assert ada_row0 % bb == 0
    return lambda b, t: (b + ada_row0 // bb, 0, 0)


def _ffn(x, ada, g, wi, wo, *, j, bb, tt, ada_row0, fin=None, g_final=None):
    B, T, d = x.shape
    dff = wo.shape[0]
    final = fin is not None
    tok = lambda b, t: (b, t, 0)
    row = _ada_rows(ada_row0, bb)
    in_specs = [pl.BlockSpec((bb, tt, d), tok),
                pl.BlockSpec((bb, N_ADA, d), row),
                _resident((1, d)), _resident((d, 2 * dff)), _resident((dff, d))]
    args = [x, ada, g.reshape(1, d), wi, wo]
    if final:
        in_specs += [pl.BlockSpec((bb, 2, d), row), _resident((1, d))]
        args += [fin, g_final.reshape(1, d)]
    return pl.pallas_call(
        functools.partial(_ffn_kernel, j=j, final=final),
        out_shape=jax.ShapeDtypeStruct((B, T, d), _F32),
        grid=(B // bb, T // tt),
        in_specs=in_specs,
        out_specs=pl.BlockSpec((bb, tt, d), tok),
        compiler_params=_params(("arbitrary", "arbitrary")),
        name="ffn_final" if final else "ffn",
    )(*args)


PREP_COLS = 256


def _prep_w_in_kernel(a_ref, b_ref, o_ref, *, n_main, n_tail):
    j = pl.program_id(0)
    blk, d = a_ref.shape
    ng = 2 * MLSTM_HEADS

    @pl.when(j < n_main)
    def _():
        o_ref[...] = a_ref[...].T.astype(_BF16)

    @pl.when(jnp.logical_and(j >= n_main, j < n_main + n_tail))
    def _():
        rows = jnp.concatenate([a_ref[ng:, :], b_ref[:ng, :]], axis=0)
        o_ref[...] = rows.T.astype(_BF16)

    @pl.when(j == n_main + n_tail)
    def _():
        rows = jnp.concatenate([a_ref[:ng, :], jnp.zeros((blk - ng, d), _F32)], axis=0)
        o_ref[...] = rows.T.astype(_BF16)


def _prep_w_in(w_t):
    n_in, d = w_t.shape
    blk = PREP_COLS
    n_main = 6 * D_MODEL // blk
    n_tail = 2 * D_MODEL // blk
    assert n_in == (n_main + n_tail) * blk + 2 * MLSTM_HEADS
    steps = n_main + n_tail + 1
    a_map = lambda j: (jnp.where(j < steps - 1, j, n_main), 0)
    b_map = lambda j: (jnp.clip(j + 1, n_main, n_main + n_tail), 0)
    return pl.pallas_call(
        functools.partial(_prep_w_in_kernel, n_main=n_main, n_tail=n_tail),
        out_shape=jax.ShapeDtypeStruct((d, steps * blk), _BF16),
        grid=(steps,),
        in_specs=[pl.BlockSpec((blk, d), a_map), pl.BlockSpec((blk, d), b_map)],
        out_specs=pl.BlockSpec((d, blk), lambda j: (0, j)),
        compiler_params=_params(("arbitrary",)),
        name="prep_w_in",
    )(w_t, w_t)


def _mix_in_kernel(x_ref, ada_ref, g_ref, w_ref, bg_ref, cw_ref, cb_ref, lng_ref, lnb_ref, wt_ref,
                   bias_ref, *rest, L, zero_hist, emit_v):
    dm = x_ref.shape[2]
    edges = [0, dm, 2 * dm, 4 * dm, 5 * dm, 6 * dm, 7 * dm, 8 * dm, 8 * dm + GATE_PAD]
    wu_ref, wv_ref, wqk_ref, wvm_ref, wo_ref, wga_ref, wgb_ref, wif_ref = [
        w_ref.at[:, lo:hi] for lo, hi in zip(edges[:-1], edges[1:])]
    rest = list(rest)
    conv0_ref = None if zero_hist else rest.pop(0)
    ya_ref, gb_ref, q_ref, k_ref, vm_ref, gcol_ref, conv_ref = rest[:7]
    rest = rest[7:]
    v_ref = rest.pop(0) if emit_v else None
    (hist_ref,) = rest

    bb, tt, d = x_ref.shape
    M = bb * tt
    t_idx = pl.program_id(1)

    h = _rms_mod(x_ref[...], g_ref[...], ada_ref[:, 3:4, :], ada_ref[:, 4:5, :])
    hb = h.reshape(M, d).astype(_BF16)

    gv = _gelu_tanh(_dot(hb, wv_ref[...]))
    mu = jnp.mean(gv, axis=-1, keepdims=True)
    var = jnp.mean(jnp.square(gv - mu), axis=-1, keepdims=True)
    v = (gv - mu) * lax.rsqrt(var + EPS) * lng_ref[...] + lnb_ref[...]
    if emit_v:
        v_ref[...] = v.reshape(bb, tt, d)
    vb = v.astype(_BF16)
    rows = lax.broadcasted_iota(jnp.int32, (M, M), 0)
    cols = lax.broadcasted_iota(jnp.int32, (M, M), 1)
    mask = jnp.logical_and(rows // L == cols // L, cols <= rows)
    CW = GMLP_GROUP_DIM
    for g in range(GMLP_GROUPS):
        sl = slice(g * CW, (g + 1) * CW)
        wg = jnp.where(mask, wt_ref[g], 0.0).astype(_BF16)
        mixed = _dot(wg, vb[:, sl]) + bias_ref[:, g:g + 1]
        u_g = _gelu_tanh(_dot(hb, wu_ref[:, sl]))
        s_g = _sigmoid(_dot(hb, wga_ref[:, sl]))
        ya_ref[:, :, sl] = (s_g * u_g * mixed).reshape(bb, tt, CW)

    nqk = cw_ref.shape[1]

    @pl.when(t_idx == 0)
    def _():
        hist_ref[...] = jnp.zeros((bb, HIST_ROWS, nqk), _F32)
        if not zero_hist:
            hist_ref[:, HIST_ROWS - (CONV_W - 1):HIST_ROWS, :] = conv0_ref[...]

    t8 = lax.broadcasted_iota(jnp.int32, (bb, HIST_ROWS, CW), 1)
    for c in range(nqk // CW):
        cs = slice(c * CW, (c + 1) * CW)
        z3 = _dot(hb, wqk_ref[:, cs]).reshape(bb, tt, CW)
        hist = hist_ref[:, :, cs]
        conv = cb_ref[:, cs] + z3 * cw_ref[CONV_W - 1:CONV_W, cs]
        for jj in range(1, CONV_W):
            rolled = pltpu.roll(z3, jj, axis=1)
            first = jnp.where(t8 < jj, pltpu.roll(hist, jj, axis=1), rolled[:, :HIST_ROWS, :])
            shifted = first if tt == HIST_ROWS else jnp.concatenate([first, rolled[:, HIST_ROWS:, :]], axis=1)
            conv = conv + shifted * cw_ref[CONV_W - 1 - jj:CONV_W - jj, cs]
        hist_ref[:, :, cs] = z3[:, tt - HIST_ROWS:, :]
        conv_ref[:, :, cs] = z3[:, tt - (CONV_W - 1):, :]
        o_ref, lo = (q_ref, c * CW) if c * CW < d else (k_ref, c * CW - d)
        o_ref[:, :, lo:lo + CW] = _silu(conv)
    for c in range(d // CW):
        cs = slice(c * CW, (c + 1) * CW)
        vm_ref[:, :, cs] = _dot(hb, wvm_ref[:, cs]).reshape(bb, tt, CW)
        gate_b = _sigmoid(_dot(hb, wo_ref[:, cs])) * _sigmoid(_dot(hb, wgb_ref[:, cs]))
        gb_ref[:, :, cs] = gate_b.reshape(bb, tt, CW)
    gif = _dot(hb, wif_ref[...])[:, :2 * MLSTM_HEADS] + bg_ref[...]
    gcol_ref[...] = gif.reshape(bb, tt, 2 * MLSTM_HEADS)


def _mix_in(x, ada, g_mix, w_in, b_gates, conv_w, conv_b, ln_g, ln_b, wtile, bias_big, conv0,
            *, bb, tt, L, ada_row0, emit_v):
    B, T, d = x.shape
    nqk = conv_w.shape[1]
    M = bb * tt
    zero_hist = conv0 is None
    tok = lambda b, t: (b, t, 0)
    row = lambda b, t: (b, 0, 0)
    in_specs = [pl.BlockSpec((bb, tt, d), tok), pl.BlockSpec((bb, N_ADA, d), _ada_rows(ada_row0, bb)),
                _resident((1, d)), _resident(w_in.shape),
                _resident((1, 2 * MLSTM_HEADS)), _resident((CONV_W, nqk)), _resident((1, nqk)),
                _resident((1, d)), _resident((1, d)), _resident((GMLP_GROUPS, M, M)),
                _resident((M, GMLP_GROUPS))]
    args = [x, ada, g_mix.reshape(1, d), w_in, b_gates.reshape(1, -1), conv_w, conv_b.reshape(1, nqk),
            ln_g.reshape(1, d), ln_b.reshape(1, d), wtile, bias_big]
    if not zero_hist:
        in_specs.append(pl.BlockSpec((bb, CONV_W - 1, nqk), row))
        args.append(conv0)
    tok_out = jax.ShapeDtypeStruct((B, T, d), _F32)
    out_shape = [tok_out] * 5 + [jax.ShapeDtypeStruct((B, T, 2 * MLSTM_HEADS), _F32),
                                 jax.ShapeDtypeStruct((B, CONV_W - 1, nqk), _F32)]
    out_specs = [pl.BlockSpec((bb, tt, d), tok)] * 5 + [
        pl.BlockSpec((bb, tt, 2 * MLSTM_HEADS), tok), pl.BlockSpec((bb, CONV_W - 1, nqk), row)]
    if emit_v:
        out_shape.append(tok_out)
        out_specs.append(pl.BlockSpec((bb, tt, d), tok))
    return pl.pallas_call(
        functools.partial(_mix_in_kernel, L=L, zero_hist=zero_hist, emit_v=emit_v),
        out_shape=out_shape,
        grid=(B // bb, T // tt),
        in_specs=in_specs,
        out_specs=out_specs,
        scratch_shapes=[pltpu.VMEM((bb, HIST_ROWS, nqk), _F32)],
        compiler_params=_params(("arbitrary", "arbitrary")),
        name="mix_in",
    )(*args)


def _mlstm_kernel(q_ref, k_ref, v_ref, gcol_ref, x_ref, ada_ref, ya_ref, gb_ref, ng_ref, wout_ref,
                  *rest, zero_state, grp):
    if zero_state:
        o_ref, C_ref, n_ref, m_ref = rest
    else:
        C0_ref, n0_ref, m0_ref, o_ref, C_ref, n_ref, m_ref = rest
    rb, L, _ = q_ref.shape
    H, DH = MLSTM_HEADS, MLSTM_HEAD_DIM

    @pl.when(pl.program_id(1) == 0)
    def _():
        if zero_state:
            C_ref[...] = jnp.zeros(C_ref.shape, _F32)
            n_ref[...] = jnp.zeros(n_ref.shape, _F32)
            m_ref[...] = jnp.zeros(m_ref.shape, _F32)
        else:
            C_ref[...] = C0_ref[...]
            n_ref[...] = n0_ref[...]
            m_ref[...] = m0_ref[...]

    ti = lax.broadcasted_iota(jnp.int32, (L, L), 0)
    si = lax.broadcasted_iota(jnp.int32, (L, L), 1)
    eye = ti == si
    causal = si <= ti

    d = x_ref.shape[2]
    NG = grp * H
    mix_rows = []
    for g0 in range(0, rb, grp):
        chains = [(r, hd) for r in range(g0, g0 + grp) for hd in range(H)]
        heads = lambda ref: jnp.stack([ref[r, :, hd * DH:(hd + 1) * DH] for r, hd in chains])
        qh = heads(q_ref)
        kh = heads(k_ref) * (DH ** -0.5)
        qb = qh.astype(_BF16)
        kb = kh.astype(_BF16)
        vb = heads(v_ref).astype(_BF16)
        Ch = C_ref[g0:g0 + grp].reshape(NG, DH, DH)
        nh = n_ref[g0:g0 + grp].reshape(NG, 1, DH)
        m_old = m_ref[g0:g0 + grp].reshape(NG, 1, GATE_PAD)[:, :, 0:1]
        gates = gcol_ref[g0:g0 + grp]
        i_col = jnp.stack([gates[r - g0, :, hd:hd + 1] for r, hd in chains])
        lf_col = _log_sigmoid(jnp.stack([gates[r - g0, :, H + hd:H + hd + 1] for r, hd in chains]))

        lf_row = jnp.sum(jnp.where(eye, lf_col, 0.0), axis=1, keepdims=True)
        b_col = jnp.sum(jnp.where(causal, lf_row, 0.0), axis=2, keepdims=True)
        c_col = i_col - b_col
        r_row = jnp.sum(jnp.where(eye, c_col, 0.0), axis=1, keepdims=True)
        b_last = b_col[:, L - 1:L, :]

        inter = b_col + m_old
        dm = jnp.where(causal, b_col + r_row, -jnp.inf)
        m_t = jnp.maximum(inter, jnp.max(dm, axis=2, keepdims=True))
        w_inter = jnp.exp(inter - m_t)
        s = _bdot(qb, kb, 2, 2) * jnp.exp(dm - m_t)
        num = w_inter * _bdot(qb, Ch.astype(_BF16), 2, 1) + _bdot(s.astype(_BF16), vb, 2, 1)
        den = (w_inter * jnp.sum(qh * nh, axis=2, keepdims=True)
               + jnp.sum(s, axis=2, keepdims=True))
        hh = num * (1.0 / jnp.maximum(jnp.abs(den), jnp.exp(-m_t)))

        m_new = jnp.maximum(b_last + m_old, jnp.max(b_last + r_row, axis=2, keepdims=True))
        a_prev = jnp.exp(b_last + m_old - m_new)
        kw = kh * jnp.exp(b_last + c_col - m_new)
        C_new = a_prev * Ch + _bdot(kw.astype(_BF16), vb, 1, 1)
        C_ref[g0:g0 + grp] = C_new.reshape(grp, H, DH, DH)
        n_ref[g0:g0 + grp] = (a_prev * nh + jnp.sum(kw, axis=1, keepdims=True)).reshape(grp, H, DH)
        m_ref[g0:g0 + grp] = jnp.broadcast_to(m_new, (NG, 1, GATE_PAD)).reshape(grp, H, GATE_PAD)

        mu = jnp.mean(hh, axis=2, keepdims=True)
        var = jnp.mean(jnp.square(hh - mu), axis=2, keepdims=True)
        ng = jnp.stack([ng_ref[:, hd * DH:(hd + 1) * DH] for _, hd in chains])
        hn = (hh - mu) * lax.rsqrt(var + EPS) * ng
        for r in range(g0, g0 + grp):
            mix_rows.append(jnp.concatenate(
                [ya_ref[r, :, hd * DH:(hd + 1) * DH]
                 + gb_ref[r, :, hd * DH:(hd + 1) * DH] * hn[(r - g0) * H + hd] for hd in range(H)], axis=1))

    mix = jnp.concatenate(mix_rows, axis=0)
    out = _dot(mix.astype(_BF16), wout_ref[...]).reshape(rb, L, d)
    o_ref[...] = x_ref[...] + ada_ref[:, 5:6, :] * out


def _mlstm(q, k, v, gcol, x, ada, ya, gb, norm_g, w_out, state, *, L, rb, grp, ada_row0):
    B, T, d = q.shape
    H, DH = MLSTM_HEADS, MLSTM_HEAD_DIM
    zero_state = state is None
    tok = lambda b, c: (b, c, 0)
    blk = pl.BlockSpec((rb, L, d), tok)
    in_specs = [blk] * 3 + [pl.BlockSpec((rb, L, 2 * H), tok), blk,
                            pl.BlockSpec((rb, N_ADA, d), _ada_rows(ada_row0, rb)), blk, blk,
                            _resident((1, d)), _resident((d, d))]
    args = [q, k, v, gcol, x, ada, ya, gb, norm_g.reshape(1, d), w_out]
    st_specs = [pl.BlockSpec((rb, H, DH, DH), lambda b, c: (b, 0, 0, 0)),
                pl.BlockSpec((rb, H, DH), lambda b, c: (b, 0, 0)),
                pl.BlockSpec((rb, H, GATE_PAD), lambda b, c: (b, 0, 0))]
    if not zero_state:
        in_specs += st_specs
        args += list(state)
    return pl.pallas_call(
        functools.partial(_mlstm_kernel, zero_state=zero_state, grp=grp),
        out_shape=[jax.ShapeDtypeStruct((B, T, d), _F32),
                   jax.ShapeDtypeStruct((B, H, DH, DH), _F32),
                   jax.ShapeDtypeStruct((B, H, DH), _F32),
                   jax.ShapeDtypeStruct((B, H, GATE_PAD), _F32)],
        grid=(B // rb, T // L),
        in_specs=in_specs,
        out_specs=[pl.BlockSpec((rb, L, d), tok)] + st_specs,
        compiler_params=_params(("arbitrary", "arbitrary")),
        name="mlstm",
    )(*args)


def _trunk(x, ada, fin, conv0, state, W, *, ada_row0, bb, tt, tt_ffn, rb, grp, emit_v):
    B, T, d = x.shape
    L = min(T, GMLP_CHUNK)
    assert L == math.gcd(T, MLSTM_CHUNK) and tt % L == 0 and tt % 8 == 0
    M = bb * tt
    wrow = jnp.tile(W["gmlp_ws"][:, :L, :L], (1, 1, M // L))
    wtile = jnp.broadcast_to(wrow[:, None], (GMLP_GROUPS, M // L, L, M)).reshape(GMLP_GROUPS, M, M)
    bias_big = jnp.tile(W["gmlp_bs"][:, :L].T, (M // L, 1))

    x = _ffn(x, ada, W["g_ffn1"], *W["ffn1"], j=0, bb=bb, tt=tt_ffn, ada_row0=ada_row0)
    outs = _mix_in(x, ada, W["g_mix"], W["w_in"], W["b_gates"], W["conv_w"], W["conv_b"],
                   W["gmlp_ln_g"], W["gmlp_ln_b"], wtile, bias_big, conv0,
                   bb=bb, tt=tt, L=L, ada_row0=ada_row0, emit_v=emit_v)
    ya, gb, q, k, vm, gcol, conv_new = outs[:7]
    x, C, n, m = _mlstm(q, k, vm, gcol, x, ada, ya, gb, W["mlstm_norm_g"], W["w_out"], state,
                        L=L, rb=rb, grp=grp, ada_row0=ada_row0)
    y = _ffn(x, ada, W["g_ffn2"], *W["ffn2"], j=6, bb=bb, tt=tt_ffn, ada_row0=ada_row0,
             fin=fin, g_final=W["g_final"])
    v = outs[7] if emit_v else None
    return y, conv_new[None], C[None], n[None], m[None, :, :, 0], v


def kernel(x_prompt, x_sample, c_prompt, c_sample, state_mlstm_C, state_mlstm_n, state_mlstm_m, state_conv, w_ada, b_ada, g_ffn1, w_ffn1_in, w_ffn1_out, g_mix, w_in, b_gates, conv_w, conv_b, gmlp_ln_g, gmlp_ln_b, gmlp_ws, gmlp_bs, mlstm_norm_g, w_out, g_ffn2, w_ffn2_in, w_ffn2_out, w_ada_final, b_ada_final, g_final):
    assert w_ada.shape[0] == 1, "single-layer trunk only"
    d, H = D_MODEL, MLSTM_HEADS
    bf = lambda a: a.astype(_BF16)

    W = {
        "ffn1": (bf(w_ffn1_in[0]), bf(w_ffn1_out[0])), "ffn2": (bf(w_ffn2_in[0]), bf(w_ffn2_out[0])),
        "w_in": _prep_w_in(w_in[0].T),
        "w_out": bf(w_out[0]),
        "g_ffn1": g_ffn1[0], "g_mix": g_mix[0], "g_ffn2": g_ffn2[0], "g_final": g_final,
        "b_gates": b_gates[0], "conv_w": conv_w[0], "conv_b": conv_b[0],
        "gmlp_ln_g": gmlp_ln_g[0], "gmlp_ln_b": gmlp_ln_b[0],
        "gmlp_ws": gmlp_ws[0], "gmlp_bs": gmlp_bs[0], "mlstm_norm_g": mlstm_norm_g[0],
    }

    Bs = x_sample.shape[0]
    c_all = jnp.concatenate([c_sample, c_prompt], axis=0)
    ada_all = _ada(c_all, w_ada[0], b_ada[0], 1024).reshape(-1, N_ADA, d)
    fin_all = _ada(c_all, w_ada_final, b_ada_final, 1024).reshape(-1, 2, d)

    y_p, conv_p, C_p, n_p, m_p, _ = _trunk(
        x_prompt, ada_all, fin_all, None, None, W,
        ada_row0=Bs, bb=1, tt=512, tt_ffn=512, rb=4, grp=1, emit_v=False)
    m0 = jnp.broadcast_to(state_mlstm_m[0][:, :, None], state_mlstm_m.shape[1:] + (GATE_PAD,))
    y_s, conv_s, C_s, n_s, m_s, v_s = _trunk(
        x_sample, ada_all, fin_all, state_conv[0],
        (state_mlstm_C[0], state_mlstm_n[0], m0), W,
        ada_row0=0, bb=32, tt=8, tt_ffn=8, rb=8, grp=8, emit_v=True)
    return (y_p, y_s, C_p, n_p, m_p, conv_p, C_s, n_s, m_s, conv_s, v_s[None])
```

```python
import functools
import math

import jax
import jax.numpy as jnp
from jax import lax
from jax.experimental import pallas as pl
from jax.experimental.pallas import tpu as pltpu

D_MODEL = 1024
N_ADA = 9
GMLP_GROUPS = 4
GMLP_GROUP_DIM = D_MODEL // GMLP_GROUPS
GMLP_CHUNK = 128
MLSTM_HEADS = 4
MLSTM_HEAD_DIM = D_MODEL // MLSTM_HEADS
MLSTM_CHUNK = 128
CONV_W = 4
EPS = 1e-6
GATE_PAD = 128
HIST_ROWS = 8
VMEM_LIMIT_BYTES = 56 * 1024 * 1024

_BF16 = jnp.bfloat16
_F32 = jnp.float32


def _dot(a, b):
    return jnp.dot(a, b, preferred_element_type=_F32)


def _dot_nt(a, b):
    return lax.dot_general(a, b, (((1,), (1,)), ((), ())), preferred_element_type=_F32)


def _bdot(a, b, ca, cb):
    return lax.dot_general(a, b, (((ca,), (cb,)), ((0,), (0,))), preferred_element_type=_F32)


def _sigmoid(x):
    return 1.0 / (1.0 + jnp.exp(-x))


def _silu(x):
    return x * _sigmoid(x)


def _gelu_tanh(x):
    return 0.5 * x * (1.0 + jnp.tanh(0.7978845608028654 * (x + 0.044715 * (x * x * x))))


def _log_sigmoid(x):
    return jnp.minimum(x, 0.0) - jnp.log1p(jnp.exp(-jnp.abs(x)))


def _rms_mod(x, g, shift, scale):
    y = x * lax.rsqrt(jnp.mean(x * x, axis=-1, keepdims=True) + EPS) * g
    return y * (1.0 + scale) + shift


def _resident(shape):
    nd = len(shape)
    return pl.BlockSpec(shape, lambda *_: (0,) * nd, pipeline_mode=pl.Buffered(1))


def _params(semantics):
    return pltpu.CompilerParams(dimension_semantics=semantics, vmem_limit_bytes=VMEM_LIMIT_BYTES)


def _ada_kernel(c_ref, w_ref, b_ref, o_ref):
    cs = _silu(c_ref[...]).astype(_BF16)
    o_ref[...] = _dot(cs, w_ref[...].astype(_BF16)) + b_ref[...]


def _ada(c, w, b, bn):
    bc, d = c.shape
    n = w.shape[1]
    return pl.pallas_call(
        _ada_kernel,
        out_shape=jax.ShapeDtypeStruct((bc, n), _F32),
        grid=(n // bn,),
        in_specs=[pl.BlockSpec((bc, d), lambda j: (0, 0)),
                  pl.BlockSpec((d, bn), lambda j: (0, j)),
                  pl.BlockSpec((1, bn), lambda j: (0, j))],
        out_specs=pl.BlockSpec((bc, bn), lambda j: (0, j)),
        compiler_params=_params(("arbitrary",)),
        name="ada",
    )(c, w, b.reshape(1, n))


def _ffn_kernel(x_ref, ada_ref, g_ref, wi_ref, wo_ref, *rest, j, final):
    if final:
        fin_ref, gf_ref, o_ref = rest
    else:
        (o_ref,) = rest
    bb, tt, d = x_ref.shape
    dff = wo_ref.shape[0]
    x = x_ref[...]
    h = _rms_mod(x, g_ref[...], ada_ref[:, j:j + 1, :], ada_ref[:, j + 1:j + 2, :])
    hb = h.reshape(bb * tt, d).astype(_BF16)
    a = _dot(hb, wi_ref[:, :dff])
    b = _dot(hb, wi_ref[:, dff:])
    gated = (_silu(a) * b).astype(_BF16)
    out = _dot(gated, wo_ref[...]).reshape(bb, tt, d)
    y = x + 0.5 * ada_ref[:, j + 2:j + 3, :] * out
    if final:
        y = _rms_mod(y, gf_ref[...], fin_ref[:, 0:1, :], fin_ref[:, 1:2, :])
    o_ref[...] = y


def _ada_rows(ada_row0, bb):
    assert ada_row0 % bb == 0
    return lambda b, t: (b + ada_row0 // bb, 0, 0)


def _ffn(x, ada, g, wi, wo, *, j, bb, tt, ada_row0, fin=None, g_final=None):
    B, T, d = x.shape
    dff = wo.shape[0]
    final = fin is not None
    tok = lambda b, t: (b, t, 0)
    row = _ada_rows(ada_row0, bb)
    in_specs = [pl.BlockSpec((bb, tt, d), tok),
                pl.BlockSpec((bb, N_ADA, d), row),
                _resident((1, d)), _resident((d, 2 * dff)), _resident((dff, d))]
    args = [x, ada, g.reshape(1, d), wi, wo]
    if final:
        in_specs += [pl.BlockSpec((bb, 2, d), row), _resident((1, d))]
        args += [fin, g_final.reshape(1, d)]
    return pl.pallas_call(
        functools.partial(_ffn_kernel, j=j, final=final),
        out_shape=jax.ShapeDtypeStruct((B, T, d), _F32),
        grid=(B // bb, T // tt),
        in_specs=in_specs,
        out_specs=pl.BlockSpec((bb, tt, d), tok),
        compiler_params=_params(("arbitrary", "arbitrary")),
        name="ffn_final" if final else "ffn",
    )(*args)


PREP_COLS = 256


def _prep_w_in_kernel(a_ref, b_ref, o_ref, *, n_main, n_tail):
    j = pl.program_id(0)
    blk, d = a_ref.shape
    ng = 2 * MLSTM_HEADS

    @pl.when(j < n_main)
    def _():
        o_ref[...] = a_ref[...].T.astype(_BF16)

    @pl.when(jnp.logical_and(j >= n_main, j < n_main + n_tail))
    def _():
        rows = jnp.concatenate([a_ref[ng:, :], b_ref[:ng, :]], axis=0)
        o_ref[...] = rows.T.astype(_BF16)

    @pl.when(j == n_main + n_tail)
    def _():
        rows = jnp.concatenate([a_ref[:ng, :], jnp.zeros((blk - ng, d), _F32)], axis=0)
        o_ref[...] = rows.T.astype(_BF16)


def _prep_w_in(w_t):
    n_in, d = w_t.shape
    blk = PREP_COLS
    n_main = 6 * D_MODEL // blk
    n_tail = 2 * D_MODEL // blk
    assert n_in == (n_main + n_tail) * blk + 2 * MLSTM_HEADS
    steps = n_main + n_tail + 1
    a_map = lambda j: (jnp.where(j < steps - 1, j, n_main), 0)
    b_map = lambda j: (jnp.clip(j + 1, n_main, n_main + n_tail), 0)
    return pl.pallas_call(
        functools.partial(_prep_w_in_kernel, n_main=n_main, n_tail=n_tail),
        out_shape=jax.ShapeDtypeStruct((d, steps * blk), _BF16),
        grid=(steps,),
        in_specs=[pl.BlockSpec((blk, d), a_map), pl.BlockSpec((blk, d), b_map)],
        out_specs=pl.BlockSpec((d, blk), lambda j: (0, j)),
        compiler_params=_params(("arbitrary",)),
        name="prep_w_in",
    )(w_t, w_t)


def _mix_in_kernel(x_ref, ada_ref, g_ref, w_ref, bg_ref, cw_ref, cb_ref, lng_ref, lnb_ref, wt_ref,
                   bias_ref, *rest, L, zero_hist, emit_v):
    dm = x_ref.shape[2]
    edges = [0, dm, 2 * dm, 4 * dm, 5 * dm, 6 * dm, 7 * dm, 8 * dm, 8 * dm + GATE_PAD]
    wu_ref, wv_ref, wqk_ref, wvm_ref, wo_ref, wga_ref, wgb_ref, wif_ref = [
        w_ref.at[:, lo:hi] for lo, hi in zip(edges[:-1], edges[1:])]
    rest = list(rest)
    conv0_ref = None if zero_hist else rest.pop(0)
    ya_ref, gb_ref, q_ref, k_ref, vm_ref, gcol_ref, conv_ref = rest[:7]
    rest = rest[7:]
    v_ref = rest.pop(0) if emit_v else None
    (hist_ref,) = rest

    bb, tt, d = x_ref.shape
    M = bb * tt
    t_idx = pl.program_id(1)

    h = _rms_mod(x_ref[...], g_ref[...], ada_ref[:, 3:4, :], ada_ref[:, 4:5, :])
    hb = h.reshape(M, d).astype(_BF16)

    gv = _gelu_tanh(_dot(hb, wv_ref[...]))
    mu = jnp.mean(gv, axis=-1, keepdims=True)
    var = jnp.mean(jnp.square(gv - mu), axis=-1, keepdims=True)
    v = (gv - mu) * lax.rsqrt(var + EPS) * lng_ref[...] + lnb_ref[...]
    if emit_v:
        v_ref[...] = v.reshape(bb, tt, d)
    vb = v.astype(_BF16)
    rows = lax.broadcasted_iota(jnp.int32, (M, M), 0)
    cols = lax.broadcasted_iota(jnp.int32, (M, M), 1)
    mask = jnp.logical_and(rows // L == cols // L, cols <= rows)
    CW = GMLP_GROUP_DIM
    for g in range(GMLP_GROUPS):
        sl = slice(g * CW, (g + 1) * CW)
        wg = jnp.where(mask, wt_ref[g], 0.0).astype(_BF16)
        mixed = _dot(wg, vb[:, sl]) + bias_ref[:, g:g + 1]
        u_g = _gelu_tanh(_dot(hb, wu_ref[:, sl]))
        s_g = _sigmoid(_dot(hb, wga_ref[:, sl]))
        ya_ref[:, :, sl] = (s_g * u_g * mixed).reshape(bb, tt, CW)

    nqk = cw_ref.shape[1]

    @pl.when(t_idx == 0)
    def _():
        hist_ref[...] = jnp.zeros((bb, HIST_ROWS, nqk), _F32)
        if not zero_hist:
            hist_ref[:, HIST_ROWS - (CONV_W - 1):HIST_ROWS, :] = conv0_ref[...]

    t8 = lax.broadcasted_iota(jnp.int32, (bb, HIST_ROWS, CW), 1)
    for c in range(nqk // CW):
        cs = slice(c * CW, (c + 1) * CW)
        z3 = _dot(hb, wqk_ref[:, cs]).reshape(bb, tt, CW)
        hist = hist_ref[:, :, cs]
        conv = cb_ref[:, cs] + z3 * cw_ref[CONV_W - 1:CONV_W, cs]
        for jj in range(1, CONV_W):
            rolled = pltpu.roll(z3, jj, axis=1)
            first = jnp.where(t8 < jj, pltpu.roll(hist, jj, axis=1), rolled[:, :HIST_ROWS, :])
            shifted = first if tt == HIST_ROWS else jnp.concatenate([first, rolled[:, HIST_ROWS:, :]], axis=1)
            conv = conv + shifted * cw_ref[CONV_W - 1 - jj:CONV_W - jj, cs]
        hist_ref[:, :, cs] = z3[:, tt - HIST_ROWS:, :]
        conv_ref[:, :, cs] = z3[:, tt - (CONV_W - 1):, :]
        o_ref, lo = (q_ref, c * CW) if c * CW < d else (k_ref, c * CW - d)
        o_ref[:, :, lo:lo + CW] = _silu(conv)
    for c in range(d // CW):
        cs = slice(c * CW, (c + 1) * CW)
        vm_ref[:, :, cs] = _dot(hb, wvm_ref[:, cs]).reshape(bb, tt, CW)
        gate_b = _sigmoid(_dot(hb, wo_ref[:, cs])) * _sigmoid(_dot(hb, wgb_ref[:, cs]))
        gb_ref[:, :, cs] = gate_b.reshape(bb, tt, CW)
    gif = _dot(hb, wif_ref[...])[:, :2 * MLSTM_HEADS] + bg_ref[...]
    gcol_ref[...] = gif.reshape(bb, tt, 2 * MLSTM_HEADS)


def _mix_in(x, ada, g_mix, w_in, b_gates, conv_w, conv_b, ln_g, ln_b, wtile, bias_big, conv0,
            *, bb, tt, L, ada_row0, emit_v):
    B, T, d = x.shape
    nqk = conv_w.shape[1]
    M = bb * tt
    zero_hist = conv0 is None
    tok = lambda b, t: (b, t, 0)
    row = lambda b, t: (b, 0, 0)
    in_specs = [pl.BlockSpec((bb, tt, d), tok), pl.BlockSpec((bb, N_ADA, d), _ada_rows(ada_row0, bb)),
                _resident((1, d)), _resident(w_in.shape),
                _resident((1, 2 * MLSTM_HEADS)), _resident((CONV_W, nqk)), _resident((1, nqk)),
                _resident((1, d)), _resident((1, d)), _resident((GMLP_GROUPS, M, M)),
                _resident((M, GMLP_GROUPS))]
    args = [x, ada, g_mix.reshape(1, d), w_in, b_gates.reshape(1, -1), conv_w, conv_b.reshape(1, nqk),
            ln_g.reshape(1, d), ln_b.reshape(1, d), wtile, bias_big]
    if not zero_hist:
        in_specs.append(pl.BlockSpec((bb, CONV_W - 1, nqk), row))
        args.append(conv0)
    tok_out = jax.ShapeDtypeStruct((B, T, d), _F32)
    out_shape = [tok_out] * 5 + [jax.ShapeDtypeStruct((B, T, 2 * MLSTM_HEADS), _F32),
                                 jax.ShapeDtypeStruct((B, CONV_W - 1, nqk), _F32)]
    out_specs = [pl.BlockSpec((bb, tt, d), tok)] * 5 + [
        pl.BlockSpec((bb, tt, 2 * MLSTM_HEADS), tok), pl.BlockSpec((bb, CONV_W - 1, nqk), row)]
    if emit_v:
        out_shape.append(tok_out)
        out_specs.append(pl.BlockSpec((bb, tt, d), tok))
    return pl.pallas_call(
        functools.partial(_mix_in_kernel, L=L, zero_hist=zero_hist, emit_v=emit_v),
        out_shape=out_shape,
        grid=(B // bb, T // tt),
        in_specs=in_specs,
        out_specs=out_specs,
        scratch_shapes=[pltpu.VMEM((bb, HIST_ROWS, nqk), _F32)],
        compiler_params=_params(("arbitrary", "arbitrary")),
        name="mix_in",
    )(*args)


def _mlstm_kernel(q_ref, k_ref, v_ref, gcol_ref, x_ref, ada_ref, ya_ref, gb_ref, ng_ref, wout_ref,
                  *rest, zero_state, grp):
    if zero_state:
        o_ref, C_ref, n_ref, m_ref = rest
    else:
        C0_ref, n0_ref, m0_ref, o_ref, C_ref, n_ref, m_ref = rest
    rb, L, _ = q_ref.shape
    H, DH = MLSTM_HEADS, MLSTM_HEAD_DIM

    @pl.when(pl.program_id(1) == 0)
    def _():
        if zero_state:
            C_ref[...] = jnp.zeros(C_ref.shape, _F32)
            n_ref[...] = jnp.zeros(n_ref.shape, _F32)
            m_ref[...] = jnp.zeros(m_ref.shape, _F32)
        else:
            C_ref[...] = C0_ref[...]
            n_ref[...] = n0_ref[...]
            m_ref[...] = m0_ref[...]

    ti = lax.broadcasted_iota(jnp.int32, (L, L), 0)
    si = lax.broadcasted_iota(jnp.int32, (L, L), 1)
    eye = ti == si
    causal = si <= ti

    d = x_ref.shape[2]
    NG = grp * H
    for g0 in range(0, rb, grp):
        chains = [(r, hd) for r in range(g0, g0 + grp) for hd in range(H)]
        heads = lambda ref: jnp.stack([ref[r, :, hd * DH:(hd + 1) * DH] for r, hd in chains])
        qh = heads(q_ref)
        kh = heads(k_ref) * (DH ** -0.5)
        qb = qh.astype(_BF16)
        kb = kh.astype(_BF16)
        vb = heads(v_ref).astype(_BF16)
        Ch = C_ref[g0:g0 + grp].reshape(NG, DH, DH)
        nh = n_ref[g0:g0 + grp].reshape(NG, 1, DH)
        m_old = m_ref[g0:g0 + grp].reshape(NG, 1, GATE_PAD)[:, :, 0:1]
        gates = gcol_ref[g0:g0 + grp]
        i_col = jnp.stack([gates[r - g0, :, hd:hd + 1] for r, hd in chains])
        lf_col = _log_sigmoid(jnp.stack([gates[r - g0, :, H + hd:H + hd + 1] for r, hd in chains]))

        lf_row = jnp.sum(jnp.where(eye, lf_col, 0.0), axis=1, keepdims=True)
        b_col = jnp.sum(jnp.where(causal, lf_row, 0.0), axis=2, keepdims=True)
        c_col = i_col - b_col
        r_row = jnp.sum(jnp.where(eye, c_col, 0.0), axis=1, keepdims=True)
        b_last = b_col[:, L - 1:L, :]

        inter = b_col + m_old
        dm = jnp.where(causal, b_col + r_row, -jnp.inf)
        m_t = jnp.maximum(inter, jnp.max(dm, axis=2, keepdims=True))
        w_inter = jnp.exp(inter - m_t)
        s = _bdot(qb, kb, 2, 2) * jnp.exp(dm - m_t)
        num = w_inter * _bdot(qb, Ch.astype(_BF16), 2, 1) + _bdot(s.astype(_BF16), vb, 2, 1)
        den = (w_inter * jnp.sum(qh * nh, axis=2, keepdims=True)
               + jnp.sum(s, axis=2, keepdims=True))
        hh = num * (1.0 / jnp.maximum(jnp.abs(den), jnp.exp(-m_t)))

        m_new = jnp.maximum(b_last + m_old, jnp.max(b_last + r_row, axis=2, keepdims=True))
        a_prev = jnp.exp(b_last + m_old - m_new)
        kw = kh * jnp.exp(b_last + c_col - m_new)
        C_new = a_prev * Ch + _bdot(kw.astype(_BF16), vb, 1, 1)
        C_ref[g0:g0 + grp] = C_new.reshape(grp, H, DH, DH)
        n_ref[g0:g0 + grp] = (a_prev * nh + jnp.sum(kw, axis=1, keepdims=True)).reshape(grp, H, DH)
        m_ref[g0:g0 + grp] = jnp.broadcast_to(m_new, (NG, 1, GATE_PAD)).reshape(grp, H, GATE_PAD)

        mu = jnp.mean(hh, axis=2, keepdims=True)
        var = jnp.mean(jnp.square(hh - mu), axis=2, keepdims=True)
        ng = jnp.stack([ng_ref[:, hd * DH:(hd + 1) * DH] for _, hd in chains])
        hn = (hh - mu) * lax.rsqrt(var + EPS) * ng
        mix = jnp.concatenate(
            [jnp.concatenate([ya_ref[r, :, hd * DH:(hd + 1) * DH]
                              + gb_ref[r, :, hd * DH:(hd + 1) * DH] * hn[(r - g0) * H + hd]
                              for hd in range(H)], axis=1) for r in range(g0, g0 + grp)], axis=0)
        out = _dot(mix.astype(_BF16), wout_ref[...]).reshape(grp, L, d)
        o_ref[g0:g0 + grp] = x_ref[g0:g0 + grp] + ada_ref[g0:g0 + grp, 5:6, :] * out


def _mlstm(q, k, v, gcol, x, ada, ya, gb, norm_g, w_out, state, *, L, rb, grp, ada_row0):
    B, T, d = q.shape
    H, DH = MLSTM_HEADS, MLSTM_HEAD_DIM
    zero_state = state is None
    tok = lambda b, c: (b, c, 0)
    blk = pl.BlockSpec((rb, L, d), tok)
    in_specs = [blk] * 3 + [pl.BlockSpec((rb, L, 2 * H), tok), blk,
                            pl.BlockSpec((rb, N_ADA, d), _ada_rows(ada_row0, rb)), blk, blk,
                            _resident((1, d)), _resident((d, d))]
    args = [q, k, v, gcol, x, ada, ya, gb, norm_g.reshape(1, d), w_out]
    st_specs = [pl.BlockSpec((rb, H, DH, DH), lambda b, c: (b, 0, 0, 0)),
                pl.BlockSpec((rb, H, DH), lambda b, c: (b, 0, 0)),
                pl.BlockSpec((rb, H, GATE_PAD), lambda b, c: (b, 0, 0))]
    if not zero_state:
        in_specs += st_specs
        args += list(state)
    return pl.pallas_call(
        functools.partial(_mlstm_kernel, zero_state=zero_state, grp=grp),
        out_shape=[jax.ShapeDtypeStruct((B, T, d), _F32),
                   jax.ShapeDtypeStruct((B, H, DH, DH), _F32),
                   jax.ShapeDtypeStruct((B, H, DH), _F32),
                   jax.ShapeDtypeStruct((B, H, GATE_PAD), _F32)],
        grid=(B // rb, T // L),
        in_specs=in_specs,
        out_specs=[pl.BlockSpec((rb, L, d), tok)] + st_specs,
        compiler_params=_params(("arbitrary", "arbitrary")),
        name="mlstm",
    )(*args)


def _trunk(x, ada, fin, conv0, state, W, *, ada_row0, bb, tt, tt_ffn, rb, grp, emit_v):
    B, T, d = x.shape
    L = min(T, GMLP_CHUNK)
    assert L == math.gcd(T, MLSTM_CHUNK) and tt % L == 0 and tt % 8 == 0
    M = bb * tt
    wrow = jnp.tile(W["gmlp_ws"][:, :L, :L], (1, 1, M // L))
    wtile = jnp.broadcast_to(wrow[:, None], (GMLP_GROUPS, M // L, L, M)).reshape(GMLP_GROUPS, M, M)
    bias_big = jnp.tile(W["gmlp_bs"][:, :L].T, (M // L, 1))

    x = _ffn(x, ada, W["g_ffn1"], *W["ffn1"], j=0, bb=bb, tt=tt_ffn, ada_row0=ada_row0)
    outs = _mix_in(x, ada, W["g_mix"], W["w_in"], W["b_gates"], W["conv_w"], W["conv_b"],
                   W["gmlp_ln_g"], W["gmlp_ln_b"], wtile, bias_big, conv0,
                   bb=bb, tt=tt, L=L, ada_row0=ada_row0, emit_v=emit_v)
    ya, gb, q, k, vm, gcol, conv_new = outs[:7]
    x, C, n, m = _mlstm(q, k, vm, gcol, x, ada, ya, gb, W["mlstm_norm_g"], W["w_out"], state,
                        L=L, rb=rb, grp=grp, ada_row0=ada_row0)
    y = _ffn(x, ada, W["g_ffn2"], *W["ffn2"], j=6, bb=bb, tt=tt_ffn, ada_row0=ada_row0,
             fin=fin, g_final=W["g_final"])
    v = outs[7] if emit_v else None
    return y, conv_new[None], C[None], n[None], m[None, :, :, 0], v


def kernel(x_prompt, x_sample, c_prompt, c_sample, state_mlstm_C, state_mlstm_n, state_mlstm_m, state_conv, w_ada, b_ada, g_ffn1, w_ffn1_in, w_ffn1_out, g_mix, w_in, b_gates, conv_w, conv_b, gmlp_ln_g, gmlp_ln_b, gmlp_ws, gmlp_bs, mlstm_norm_g, w_out, g_ffn2, w_ffn2_in, w_ffn2_out, w_ada_final, b_ada_final, g_final):
    assert w_ada.shape[0] == 1, "single-layer trunk only"
    d, H = D_MODEL, MLSTM_HEADS
    bf = lambda a: a.astype(_BF16)

    W = {
        "ffn1": (bf(w_ffn1_in[0]), bf(w_ffn1_out[0])), "ffn2": (bf(w_ffn2_in[0]), bf(w_ffn2_out[0])),
        "w_in": _prep_w_in(w_in[0].T),
        "w_out": bf(w_out[0]),
        "g_ffn1": g_ffn1[0], "g_mix": g_mix[0], "g_ffn2": g_ffn2[0], "g_final": g_final,
        "b_gates": b_gates[0], "conv_w": conv_w[0], "conv_b": conv_b[0],
        "gmlp_ln_g": gmlp_ln_g[0], "gmlp_ln_b": gmlp_ln_b[0],
        "gmlp_ws": gmlp_ws[0], "gmlp_bs": gmlp_bs[0], "mlstm_norm_g": mlstm_norm_g[0],
    }

    Bs = x_sample.shape[0]
    c_all = jnp.concatenate([c_sample, c_prompt], axis=0)
    ada_all = _ada(c_all, w_ada[0], b_ada[0], 1024).reshape(-1, N_ADA, d)
    fin_all = _ada(c_all, w_ada_final, b_ada_final, 1024).reshape(-1, 2, d)

    y_p, conv_p, C_p, n_p, m_p, _ = _trunk(
        x_prompt, ada_all, fin_all, None, None, W,
        ada_row0=Bs, bb=1, tt=512, tt_ffn=512, rb=4, grp=1, emit_v=False)
    m0 = jnp.broadcast_to(state_mlstm_m[0][:, :, None], state_mlstm_m.shape[1:] + (GATE_PAD,))
    y_s, conv_s, C_s, n_s, m_s, v_s = _trunk(
        x_sample, ada_all, fin_all, state_conv[0],
        (state_mlstm_C[0], state_mlstm_n[0], m0), W,
        ada_row0=0, bb=32, tt=8, tt_ffn=8, rb=8, grp=8, emit_v=True)
    return (y_p, y_s, C_p, n_p, m_p, conv_p, C_s, n_s, m_s, conv_s, v_s[None])
```

```python
import functools
import math

import jax
import jax.numpy as jnp
from jax import lax
from jax.experimental import pallas as pl
from jax.experimental.pallas import tpu as pltpu

D_MODEL = 1024
N_ADA = 9
GMLP_GROUPS = 4
GMLP_GROUP_DIM = D_MODEL // GMLP_GROUPS
GMLP_CHUNK = 128
MLSTM_HEADS = 4
MLSTM_HEAD_DIM = D_MODEL // MLSTM_HEADS
MLSTM_CHUNK = 128
CONV_W = 4
EPS = 1e-6
GATE_PAD = 128
HIST_ROWS = 8
VMEM_LIMIT_BYTES = 56 * 1024 * 1024

_BF16 = jnp.bfloat16
_F32 = jnp.float32


def _dot(a, b):
    return jnp.dot(a, b, preferred_element_type=_F32)


def _dot_nt(a, b):
    return lax.dot_general(a, b, (((1,), (1,)), ((), ())), preferred_element_type=_F32)


def _bdot(a, b, ca, cb):
    return lax.dot_general(a, b, (((ca,), (cb,)), ((0,), (0,))), preferred_element_type=_F32)


def _sigmoid(x):
    return 1.0 / (1.0 + jnp.exp(-x))


def _silu(x):
    return x * _sigmoid(x)


def _gelu_tanh(x):
    return 0.5 * x * (1.0 + jnp.tanh(0.7978845608028654 * (x + 0.044715 * (x * x * x))))


def _log_sigmoid(x):
    return jnp.minimum(x, 0.0) - jnp.log1p(jnp.exp(-jnp.abs(x)))


def _rms_mod(x, g, shift, scale):
    y = x * lax.rsqrt(jnp.mean(x * x, axis=-1, keepdims=True) + EPS) * g
    return y * (1.0 + scale) + shift


def _resident(shape):
    nd = len(shape)
    return pl.BlockSpec(shape, lambda *_: (0,) * nd, pipeline_mode=pl.Buffered(1))


def _params(semantics):
    return pltpu.CompilerParams(dimension_semantics=semantics, vmem_limit_bytes=VMEM_LIMIT_BYTES)


def _ada_kernel(c_ref, w_ref, b_ref, o_ref):
    cs = _silu(c_ref[...]).astype(_BF16)
    o_ref[...] = _dot(cs, w_ref[...].astype(_BF16)) + b_ref[...]


def _ada(c, w, b, bn):
    bc, d = c.shape
    n = w.shape[1]
    return pl.pallas_call(
        _ada_kernel,
        out_shape=jax.ShapeDtypeStruct((bc, n), _F32),
        grid=(n // bn,),
        in_specs=[pl.BlockSpec((bc, d), lambda j: (0, 0)),
                  pl.BlockSpec((d, bn), lambda j: (0, j)),
                  pl.BlockSpec((1, bn), lambda j: (0, j))],
        out_specs=pl.BlockSpec((bc, bn), lambda j: (0, j)),
        compiler_params=_params(("arbitrary",)),
        name="ada",
    )(c, w, b.reshape(1, n))


def _ffn_kernel(x_ref, ada_ref, g_ref, wi_ref, wo_ref, *rest, j, final):
    if final:
        fin_ref, gf_ref, o_ref = rest
    else:
        (o_ref,) = rest
    bb, tt, d = x_ref.shape
    dff = wo_ref.shape[0]
    x = x_ref[...]
    h = _rms_mod(x, g_ref[...], ada_ref[:, j:j + 1, :], ada_ref[:, j + 1:j + 2, :])
    hb = h.reshape(bb * tt, d).astype(_BF16)
    a = _dot(hb, wi_ref[:, :dff])
    b = _dot(hb, wi_ref[:, dff:])
    gated = (_silu(a) * b).astype(_BF16)
    out = _dot(gated, wo_ref[...]).reshape(bb, tt, d)
    y = x + 0.5 * ada_ref[:, j + 2:j + 3, :] * out
    if final:
        y = _rms_mod(y, gf_ref[...], fin_ref[:, 0:1, :], fin_ref[:, 1:2, :])
    o_ref[...] = y


def _ada_rows(ada_row0, bb):
    assert ada_row0 % bb == 0
    return lambda b, t: (b + ada_row0 // bb, 0, 0)


def _ffn(x, ada, g, wi, wo, *, j, bb, tt, ada_row0, fin=None, g_final=None):
    B, T, d = x.shape
    dff = wo.shape[0]
    final = fin is not None
    tok = lambda b, t: (b, t, 0)
    row = _ada_rows(ada_row0, bb)
    in_specs = [pl.BlockSpec((bb, tt, d), tok),
                pl.BlockSpec((bb, N_ADA, d), row),
                _resident((1, d)), _resident((d, 2 * dff)), _resident((dff, d))]
    args = [x, ada, g.reshape(1, d), wi, wo]
    if final:
        in_specs += [pl.BlockSpec((bb, 2, d), row), _resident((1, d))]
        args += [fin, g_final.reshape(1, d)]
    return pl.pallas_call(
        functools.partial(_ffn_kernel, j=j, final=final),
        out_shape=jax.ShapeDtypeStruct((B, T, d), _F32),
        grid=(B // bb, T // tt),
        in_specs=in_specs,
        out_specs=pl.BlockSpec((bb, tt, d), tok),
        compiler_params=_params(("arbitrary", "arbitrary")),
        name="ffn_final" if final else "ffn",
    )(*args)


PREP_COLS = 512


def _prep_w_in_kernel(a_ref, b_ref, o_ref, *, n_main, n_tail):
    j = pl.program_id(0)
    blk, d = a_ref.shape
    ng = 2 * MLSTM_HEADS

    @pl.when(j < n_main)
    def _():
        o_ref[...] = a_ref[...].T.astype(_BF16)

    @pl.when(jnp.logical_and(j >= n_main, j < n_main + n_tail))
    def _():
        rows = jnp.concatenate([a_ref[ng:, :], b_ref[:ng, :]], axis=0)
        o_ref[...] = rows.T.astype(_BF16)

    @pl.when(j == n_main + n_tail)
    def _():
        rows = jnp.concatenate([a_ref[:ng, :], jnp.zeros((blk - ng, d), _F32)], axis=0)
        o_ref[...] = rows.T.astype(_BF16)


def _prep_w_in(w_t):
    n_in, d = w_t.shape
    blk = PREP_COLS
    n_main = 6 * D_MODEL // blk
    n_tail = 2 * D_MODEL // blk
    assert n_in == (n_main + n_tail) * blk + 2 * MLSTM_HEADS
    steps = n_main + n_tail + 1
    a_map = lambda j: (jnp.where(j < steps - 1, j, n_main), 0)
    b_map = lambda j: (jnp.clip(j + 1, n_main, n_main + n_tail), 0)
    return pl.pallas_call(
        functools.partial(_prep_w_in_kernel, n_main=n_main, n_tail=n_tail),
        out_shape=jax.ShapeDtypeStruct((d, steps * blk), _BF16),
        grid=(steps,),
        in_specs=[pl.BlockSpec((blk, d), a_map), pl.BlockSpec((blk, d), b_map)],
        out_specs=pl.BlockSpec((d, blk), lambda j: (0, j)),
        compiler_params=_params(("arbitrary",)),
        name="prep_w_in",
    )(w_t, w_t)


def _mix_in_kernel(x_ref, ada_ref, g_ref, w_ref, bg_ref, cw_ref, cb_ref, lng_ref, lnb_ref, wt_ref,
                   bias_ref, *rest, L, zero_hist, emit_v):
    dm = x_ref.shape[2]
    edges = [0, dm, 2 * dm, 4 * dm, 5 * dm, 6 * dm, 7 * dm, 8 * dm, 8 * dm + GATE_PAD]
    wu_ref, wv_ref, wqk_ref, wvm_ref, wo_ref, wga_ref, wgb_ref, wif_ref = [
        w_ref.at[:, lo:hi] for lo, hi in zip(edges[:-1], edges[1:])]
    rest = list(rest)
    conv0_ref = None if zero_hist else rest.pop(0)
    ya_ref, gb_ref, q_ref, k_ref, vm_ref, gcol_ref, conv_ref = rest[:7]
    rest = rest[7:]
    v_ref = rest.pop(0) if emit_v else None
    (hist_ref,) = rest

    bb, tt, d = x_ref.shape
    M = bb * tt
    t_idx = pl.program_id(1)

    h = _rms_mod(x_ref[...], g_ref[...], ada_ref[:, 3:4, :], ada_ref[:, 4:5, :])
    hb = h.reshape(M, d).astype(_BF16)

    gv = _gelu_tanh(_dot(hb, wv_ref[...]))
    mu = jnp.mean(gv, axis=-1, keepdims=True)
    var = jnp.mean(jnp.square(gv - mu), axis=-1, keepdims=True)
    v = (gv - mu) * lax.rsqrt(var + EPS) * lng_ref[...] + lnb_ref[...]
    if emit_v:
        v_ref[...] = v.reshape(bb, tt, d)
    vb = v.astype(_BF16)
    rows = lax.broadcasted_iota(jnp.int32, (M, M), 0)
    cols = lax.broadcasted_iota(jnp.int32, (M, M), 1)
    mask = jnp.logical_and(rows // L == cols // L, cols <= rows)
    CW = GMLP_GROUP_DIM
    for g in range(GMLP_GROUPS):
        sl = slice(g * CW, (g + 1) * CW)
        wg = jnp.where(mask, wt_ref[g], 0.0).astype(_BF16)
        mixed = _dot(wg, vb[:, sl]) + bias_ref[:, g:g + 1]
        u_g = _gelu_tanh(_dot(hb, wu_ref[:, sl]))
        s_g = _sigmoid(_dot(hb, wga_ref[:, sl]))
        ya_ref[:, :, sl] = (s_g * u_g * mixed).reshape(bb, tt, CW)

    nqk = cw_ref.shape[1]

    @pl.when(t_idx == 0)
    def _():
        hist_ref[...] = jnp.zeros((bb, HIST_ROWS, nqk), _F32)
        if not zero_hist:
            hist_ref[:, HIST_ROWS - (CONV_W - 1):HIST_ROWS, :] = conv0_ref[...]

    t8 = lax.broadcasted_iota(jnp.int32, (bb, HIST_ROWS, CW), 1)
    for c in range(nqk // CW):
        cs = slice(c * CW, (c + 1) * CW)
        z3 = _dot(hb, wqk_ref[:, cs]).reshape(bb, tt, CW)
        hist = hist_ref[:, :, cs]
        conv = cb_ref[:, cs] + z3 * cw_ref[CONV_W - 1:CONV_W, cs]
        for jj in range(1, CONV_W):
            rolled = pltpu.roll(z3, jj, axis=1)
            first = jnp.where(t8 < jj, pltpu.roll(hist, jj, axis=1), rolled[:, :HIST_ROWS, :])
            shifted = first if tt == HIST_ROWS else jnp.concatenate([first, rolled[:, HIST_ROWS:, :]], axis=1)
            conv = conv + shifted * cw_ref[CONV_W - 1 - jj:CONV_W - jj, cs]
        hist_ref[:, :, cs] = z3[:, tt - HIST_ROWS:, :]
        conv_ref[:, :, cs] = z3[:, tt - (CONV_W - 1):, :]
        o_ref, lo = (q_ref, c * CW) if c * CW < d else (k_ref, c * CW - d)
        o_ref[:, :, lo:lo + CW] = _silu(conv)
    for c in range(d // CW):
        cs = slice(c * CW, (c + 1) * CW)
        vm_ref[:, :, cs] = _dot(hb, wvm_ref[:, cs]).reshape(bb, tt, CW)
        gate_b = _sigmoid(_dot(hb, wo_ref[:, cs])) * _sigmoid(_dot(hb, wgb_ref[:, cs]))
        gb_ref[:, :, cs] = gate_b.reshape(bb, tt, CW)
    gif = _dot(hb, wif_ref[...])[:, :2 * MLSTM_HEADS] + bg_ref[...]
    gcol_ref[...] = gif.reshape(bb, tt, 2 * MLSTM_HEADS)


def _mix_in(x, ada, g_mix, w_in, b_gates, conv_w, conv_b, ln_g, ln_b, wtile, bias_big, conv0,
            *, bb, tt, L, ada_row0, emit_v):
    B, T, d = x.shape
    nqk = conv_w.shape[1]
    M = bb * tt
    zero_hist = conv0 is None
    tok = lambda b, t: (b, t, 0)
    row = lambda b, t: (b, 0, 0)
    in_specs = [pl.BlockSpec((bb, tt, d), tok), pl.BlockSpec((bb, N_ADA, d), _ada_rows(ada_row0, bb)),
                _resident((1, d)), _resident(w_in.shape),
                _resident((1, 2 * MLSTM_HEADS)), _resident((CONV_W, nqk)), _resident((1, nqk)),
                _resident((1, d)), _resident((1, d)), _resident((GMLP_GROUPS, M, M)),
                _resident((M, GMLP_GROUPS))]
    args = [x, ada, g_mix.reshape(1, d), w_in, b_gates.reshape(1, -1), conv_w, conv_b.reshape(1, nqk),
            ln_g.reshape(1, d), ln_b.reshape(1, d), wtile, bias_big]
    if not zero_hist:
        in_specs.append(pl.BlockSpec((bb, CONV_W - 1, nqk), row))
        args.append(conv0)
    tok_out = jax.ShapeDtypeStruct((B, T, d), _F32)
    out_shape = [tok_out] * 5 + [jax.ShapeDtypeStruct((B, T, 2 * MLSTM_HEADS), _F32),
                                 jax.ShapeDtypeStruct((B, CONV_W - 1, nqk), _F32)]
    out_specs = [pl.BlockSpec((bb, tt, d), tok)] * 5 + [
        pl.BlockSpec((bb, tt, 2 * MLSTM_HEADS), tok), pl.BlockSpec((bb, CONV_W - 1, nqk), row)]
    if emit_v:
        out_shape.append(tok_out)
        out_specs.append(pl.BlockSpec((bb, tt, d), tok))
    return pl.pallas_call(
        functools.partial(_mix_in_kernel, L=L, zero_hist=zero_hist, emit_v=emit_v),
        out_shape=out_shape,
        grid=(B // bb, T // tt),
        in_specs=in_specs,
        out_specs=out_specs,
        scratch_shapes=[pltpu.VMEM((bb, HIST_ROWS, nqk), _F32)],
        compiler_params=_params(("arbitrary", "arbitrary")),
        name="mix_in",
    )(*args)


def _mlstm_kernel(q_ref, k_ref, v_ref, gcol_ref, x_ref, ada_ref, ya_ref, gb_ref, ng_ref, wout_ref,
                  *rest, zero_state, grp):
    if zero_state:
        o_ref, C_ref, n_ref, m_ref = rest
    else:
        C0_ref, n0_ref, m0_ref, o_ref, C_ref, n_ref, m_ref = rest
    rb, L, _ = q_ref.shape
    H, DH = MLSTM_HEADS, MLSTM_HEAD_DIM

    @pl.when(pl.program_id(1) == 0)
    def _():
        if zero_state:
            C_ref[...] = jnp.zeros(C_ref.shape, _F32)
            n_ref[...] = jnp.zeros(n_ref.shape, _F32)
            m_ref[...] = jnp.zeros(m_ref.shape, _F32)
        else:
            C_ref[...] = C0_ref[...]
            n_ref[...] = n0_ref[...]
            m_ref[...] = m0_ref[...]

    ti = lax.broadcasted_iota(jnp.int32, (L, L), 0)
    si = lax.broadcasted_iota(jnp.int32, (L, L), 1)
    eye = ti == si
    causal = si <= ti

    d = x_ref.shape[2]
    NG = grp * H
    for g0 in range(0, rb, grp):
        chains = [(r, hd) for r in range(g0, g0 + grp) for hd in range(H)]
        heads = lambda ref: jnp.stack([ref[r, :, hd * DH:(hd + 1) * DH] for r, hd in chains])
        qh = heads(q_ref)
        kh = heads(k_ref) * (DH ** -0.5)
        qb = qh.astype(_BF16)
        kb = kh.astype(_BF16)
        vb = heads(v_ref).astype(_BF16)
        Ch = C_ref[g0:g0 + grp].reshape(NG, DH, DH)
        nh = n_ref[g0:g0 + grp].reshape(NG, 1, DH)
        m_old = m_ref[g0:g0 + grp].reshape(NG, 1, GATE_PAD)[:, :, 0:1]
        gates = gcol_ref[g0:g0 + grp]
        i_col = jnp.stack([gates[r - g0, :, hd:hd + 1] for r, hd in chains])
        lf_col = _log_sigmoid(jnp.stack([gates[r - g0, :, H + hd:H + hd + 1] for r, hd in chains]))

        lf_row = jnp.sum(jnp.where(eye, lf_col, 0.0), axis=1, keepdims=True)
        b_col = jnp.sum(jnp.where(causal, lf_row, 0.0), axis=2, keepdims=True)
        c_col = i_col - b_col
        r_row = jnp.sum(jnp.where(eye, c_col, 0.0), axis=1, keepdims=True)
        b_last = b_col[:, L - 1:L, :]

        inter = b_col + m_old
        dm = jnp.where(causal, b_col + r_row, -jnp.inf)
        m_t = jnp.maximum(inter, jnp.max(dm, axis=2, keepdims=True))
        w_inter = jnp.exp(inter - m_t)
        s = _bdot(qb, kb, 2, 2) * jnp.exp(dm - m_t)
        num = w_inter * _bdot(qb, Ch.astype(_BF16), 2, 1) + _bdot(s.astype(_BF16), vb, 2, 1)
        den = (w_inter * jnp.sum(qh * nh, axis=2, keepdims=True)
               + jnp.sum(s, axis=2, keepdims=True))
        hh = num * (1.0 / jnp.maximum(jnp.abs(den), jnp.exp(-m_t)))

        m_new = jnp.maximum(b_last + m_old, jnp.max(b_last + r_row, axis=2, keepdims=True))
        a_prev = jnp.exp(b_last + m_old - m_new)
        kw = kh * jnp.exp(b_last + c_col - m_new)
        C_new = a_prev * Ch + _bdot(kw.astype(_BF16), vb, 1, 1)
        C_ref[g0:g0 + grp] = C_new.reshape(grp, H, DH, DH)
        n_ref[g0:g0 + grp] = (a_prev * nh + jnp.sum(kw, axis=1, keepdims=True)).reshape(grp, H, DH)
        m_ref[g0:g0 + grp] = jnp.broadcast_to(m_new, (NG, 1, GATE_PAD)).reshape(grp, H, GATE_PAD)

        mu = jnp.mean(hh, axis=2, keepdims=True)
        var = jnp.mean(jnp.square(hh - mu), axis=2, keepdims=True)
        ng = jnp.stack([ng_ref[:, hd * DH:(hd + 1) * DH] for _, hd in chains])
        hn = (hh - mu) * lax.rsqrt(var + EPS) * ng
        mix = jnp.concatenate(
            [jnp.concatenate([ya_ref[r, :, hd * DH:(hd + 1) * DH]
                              + gb_ref[r, :, hd * DH:(hd + 1) * DH] * hn[(r - g0) * H + hd]
                              for hd in range(H)], axis=1) for r in range(g0, g0 + grp)], axis=0)
        out = _dot(mix.astype(_BF16), wout_ref[...]).reshape(grp, L, d)
        o_ref[g0:g0 + grp] = x_ref[g0:g0 + grp] + ada_ref[g0:g0 + grp, 5:6, :] * out


def _mlstm(q, k, v, gcol, x, ada, ya, gb, norm_g, w_out, state, *, L, rb, grp, ada_row0):
    B, T, d = q.shape
    H, DH = MLSTM_HEADS, MLSTM_HEAD_DIM
    zero_state = state is None
    tok = lambda b, c: (b, c, 0)
    blk = pl.BlockSpec((rb, L, d), tok)
    in_specs = [blk] * 3 + [pl.BlockSpec((rb, L, 2 * H), tok), blk,
                            pl.BlockSpec((rb, N_ADA, d), _ada_rows(ada_row0, rb)), blk, blk,
                            _resident((1, d)), _resident((d, d))]
    args = [q, k, v, gcol, x, ada, ya, gb, norm_g.reshape(1, d), w_out]
    st_specs = [pl.BlockSpec((rb, H, DH, DH), lambda b, c: (b, 0, 0, 0)),
                pl.BlockSpec((rb, H, DH), lambda b, c: (b, 0, 0)),
                pl.BlockSpec((rb, H, GATE_PAD), lambda b, c: (b, 0, 0))]
    if not zero_state:
        in_specs += st_specs
        args += list(state)
    return pl.pallas_call(
        functools.partial(_mlstm_kernel, zero_state=zero_state, grp=grp),
        out_shape=[jax.ShapeDtypeStruct((B, T, d), _F32),
                   jax.ShapeDtypeStruct((B, H, DH, DH), _F32),
                   jax.ShapeDtypeStruct((B, H, DH), _F32),
                   jax.ShapeDtypeStruct((B, H, GATE_PAD), _F32)],
        grid=(B // rb, T // L),
        in_specs=in_specs,
        out_specs=[pl.BlockSpec((rb, L, d), tok)] + st_specs,
        compiler_params=_params(("arbitrary", "arbitrary")),
        name="mlstm",
    )(*args)


def _trunk(x, ada, fin, conv0, state, W, *, ada_row0, bb, tt, tt_ffn, rb, grp, emit_v):
    B, T, d = x.shape
    L = min(T, GMLP_CHUNK)
    assert L == math.gcd(T, MLSTM_CHUNK) and tt % L == 0 and tt % 8 == 0
    M = bb * tt
    wrow = jnp.tile(W["gmlp_ws"][:, :L, :L], (1, 1, M // L))
    wtile = jnp.broadcast_to(wrow[:, None], (GMLP_GROUPS, M // L, L, M)).reshape(GMLP_GROUPS, M, M)
    bias_big = jnp.tile(W["gmlp_bs"][:, :L].T, (M // L, 1))

    x = _ffn(x, ada, W["g_ffn1"], *W["ffn1"], j=0, bb=bb, tt=tt_ffn, ada_row0=ada_row0)
    outs = _mix_in(x, ada, W["g_mix"], W["w_in"], W["b_gates"], W["conv_w"], W["conv_b"],
                   W["gmlp_ln_g"], W["gmlp_ln_b"], wtile, bias_big, conv0,
                   bb=bb, tt=tt, L=L, ada_row0=ada_row0, emit_v=emit_v)
    ya, gb, q, k, vm, gcol, conv_new = outs[:7]
    x, C, n, m = _mlstm(q, k, vm, gcol, x, ada, ya, gb, W["mlstm_norm_g"], W["w_out"], state,
                        L=L, rb=rb, grp=grp, ada_row0=ada_row0)
    y = _ffn(x, ada, W["g_ffn2"], *W["ffn2"], j=6, bb=bb, tt=tt_ffn, ada_row0=ada_row0,
             fin=fin, g_final=W["g_final"])
    v = outs[7] if emit_v else None
    return y, conv_new[None], C[None], n[None], m[None, :, :, 0], v


def kernel(x_prompt, x_sample, c_prompt, c_sample, state_mlstm_C, state_mlstm_n, state_mlstm_m, state_conv, w_ada, b_ada, g_ffn1, w_ffn1_in, w_ffn1_out, g_mix, w_in, b_gates, conv_w, conv_b, gmlp_ln_g, gmlp_ln_b, gmlp_ws, gmlp_bs, mlstm_norm_g, w_out, g_ffn2, w_ffn2_in, w_ffn2_out, w_ada_final, b_ada_final, g_final):
    assert w_ada.shape[0] == 1, "single-layer trunk only"
    d, H = D_MODEL, MLSTM_HEADS
    bf = lambda a: a.astype(_BF16)

    W = {
        "ffn1": (bf(w_ffn1_in[0]), bf(w_ffn1_out[0])), "ffn2": (bf(w_ffn2_in[0]), bf(w_ffn2_out[0])),
        "w_in": _prep_w_in(w_in[0].T),
        "w_out": bf(w_out[0]),
        "g_ffn1": g_ffn1[0], "g_mix": g_mix[0], "g_ffn2": g_ffn2[0], "g_final": g_final,
        "b_gates": b_gates[0], "conv_w": conv_w[0], "conv_b": conv_b[0],
        "gmlp_ln_g": gmlp_ln_g[0], "gmlp_ln_b": gmlp_ln_b[0],
        "gmlp_ws": gmlp_ws[0], "gmlp_bs": gmlp_bs[0], "mlstm_norm_g": mlstm_norm_g[0],
    }

    Bs = x_sample.shape[0]
    c_all = jnp.concatenate([c_sample, c_prompt], axis=0)
    ada_all = _ada(c_all, w_ada[0], b_ada[0], 1024).reshape(-1, N_ADA, d)
    fin_all = _ada(c_all, w_ada_final, b_ada_final, 1024).reshape(-1, 2, d)

    y_p, conv_p, C_p, n_p, m_p, _ = _trunk(
        x_prompt, ada_all, fin_all, None, None, W,
        ada_row0=Bs, bb=1, tt=512, tt_ffn=512, rb=4, grp=1, emit_v=False)
    m0 = jnp.broadcast_to(state_mlstm_m[0][:, :, None], state_mlstm_m.shape[1:] + (GATE_PAD,))
    y_s, conv_s, C_s, n_s, m_s, v_s = _trunk(
        x_sample, ada_all, fin_all, state_conv[0],
        (state_mlstm_C[0], state_mlstm_n[0], m0), W,
        ada_row0=0, bb=32, tt=8, tt_ffn=8, rb=8, grp=8, emit_v=True)
    return (y_p, y_s, C_p, n_p, m_p, conv_p, C_s, n_s, m_s, conv_s, v_s[None])
```
